```python
import math
import jax, jax.numpy as jnp
from jax import lax
import numpy as np

D_MODEL = 1024
BATCH = 32
SEQ = 2048
DEPTH = 1
DEC_BATCH = 4
DEC_SEQ = 4096
PAST_LEN = 128

GRID_W = 64
WIN_R = 8
WIN_C = 16
ATT_HEADS = 16
HEAD_DIM = 64
ATT_W = ATT_HEADS * HEAD_DIM
SSD_INNER = 2 * D_MODEL
SSD_HEAD_DIM = 64
SSD_HEADS = SSD_INNER // SSD_HEAD_DIM
SSD_GROUPS = 8
SSD_HPG = SSD_HEADS // SSD_GROUPS
SSD_STATE = 128
SSD_CONV = 5
CONV_DIM = SSD_INNER + 2 * SSD_GROUPS * SSD_STATE
CHUNK = 128
D_FF = 2816
FFN_CONV = 3
IN_SIZES = (ATT_W, ATT_W, ATT_W, SSD_INNER, CONV_DIM, SSD_HEADS, SSD_HEADS, D_MODEL, D_MODEL)
IN_W = 3 * ATT_W + SSD_INNER + CONV_DIM + 2 * SSD_HEADS + 2 * D_MODEL
EPS = 1e-6

kernel_name = "hybrid_natten_ssd_encoder"


def rmsnorm(x, g):
    xf = x.astype(jnp.float32)
    y = xf * lax.rsqrt(jnp.mean(xf * xf, axis=-1, keepdims=True) + EPS)
    return (y * g.astype(jnp.float32)).astype(x.dtype)


def split_cols(t, sizes):
    return jnp.split(t, np.cumsum(np.array(sizes))[:-1].tolist(), axis=-1)


def dwconv_centered(u, w, bias):
    k = w.shape[0]
    out = lax.conv_general_dilated(u, w[:, None, :], window_strides=(1,),
                                   padding=[(k // 2, k // 2)],
                                   dimension_numbers=('NWC', 'WIO', 'NWC'),
                                   feature_group_count=u.shape[-1])
    return out + bias


def neighborhood_attention_2d(q, k, v, rel_bias):
    b, L, H, Dh = q.shape
    rows = L // GRID_W
    kr = min(WIN_R, rows)
    n_cb = GRID_W // WIN_C
    band = 2 * WIN_C
    qg = q.reshape(b, rows, GRID_W, H, Dh)
    kg = k.reshape(b, rows, GRID_W, H, Dh)
    vg = v.reshape(b, rows, GRID_W, H, Dh)
    qcol = np.arange(GRID_W).reshape(n_cb, WIN_C)
    col_start = np.clip(qcol - WIN_C // 2, 0, GRID_W - WIN_C)
    band_start = np.clip(np.arange(n_cb) * WIN_C - WIN_C // 2, 0, GRID_W - band)
    band_cols = band_start[:, None] + np.arange(band)
    kc = band_cols[:, None, :]
    col_ok = jnp.asarray((kc >= col_start[..., None]) & (kc < col_start[..., None] + WIN_C))
    col_bias_idx = np.clip(kc - qcol[:, :, None] + WIN_C - 1, 0, 2 * WIN_C - 2)
    k_cols = kg[:, :, band_cols]
    v_cols = vg[:, :, band_cols]
    row_start = jnp.clip(jnp.arange(rows) - WIN_R // 2, 0, rows - kr)

    def one_row(r):
        rs = row_start[r]
        kb = lax.dynamic_slice_in_dim(k_cols, rs, kr, axis=1)
        vb = lax.dynamic_slice_in_dim(v_cols, rs, kr, axis=1)
        qr = lax.dynamic_index_in_dim(qg, r, axis=1, keepdims=False).reshape(b, n_cb, WIN_C, H, Dh)
        s = jnp.einsum('bmqhd,bimkhd->bhmqik', qr, kb).astype(jnp.float32)
        row_idx = rs + jnp.arange(kr) - r + WIN_R - 1
        bias = rel_bias[:, row_idx][:, :, col_bias_idx]
        bias = jnp.transpose(bias, (0, 2, 3, 1, 4)).astype(jnp.float32)
        s = jnp.where(col_ok[:, :, None, :], s + bias, -jnp.inf)
        p = jax.nn.softmax(s.reshape(b, H, n_cb, WIN_C, kr * band), axis=-1)
        p = p.reshape(b, H, n_cb, WIN_C, kr, band).astype(v.dtype)
        o = jnp.einsum('bhmqik,bimkhd->bmqhd', p, vb)
        return o.reshape(b, GRID_W, H, Dh)

    out = lax.map(one_row, jnp.arange(rows))
    return jnp.moveaxis(out, 0, 1).reshape(b, L, H * Dh)


def ssd_chunked(x, dt, A, B, C):
    b, L, G, R, P = x.shape
    N = B.shape[-1]
    nc = L // CHUNK
    xq = x.reshape(b, nc, CHUNK, G, R, P)
    dtq = dt.reshape(b, nc, CHUNK, G, R)
    Bq = B.reshape(b, nc, CHUNK, G, N)
    Cq = C.reshape(b, nc, CHUNK, G, N)
    acs = jnp.cumsum(dtq * A, axis=2)
    xdt = xq * dtq[..., None]
    causal = jnp.tril(jnp.ones((CHUNK, CHUNK), bool))[None, None, :, :, None, None]
    seg = acs[:, :, :, None] - acs[:, :, None, :]
    decay = jnp.exp(jnp.where(causal, seg, -jnp.inf))
    scores = jnp.einsum('bcign,bcjgn->bcijg', Cq, Bq)
    y_diag = jnp.einsum('bcijgr,bcjgrp->bcigrp', scores[..., None] * decay, xdt)
    decay_to_end = jnp.exp(acs[:, :, -1:] - acs)
    chunk_states = jnp.einsum('bcjgn,bcjgrp->bcgrpn', Bq, xdt * decay_to_end[..., None])
    chunk_decay = jnp.exp(acs[:, :, -1])

    def step(h, inp):
        s, d = inp
        return h * d[..., None, None] + s, h

    h0 = jnp.zeros((b, G, R, P, N), jnp.float32)
    _, prev = lax.scan(step, h0, (jnp.moveaxis(chunk_states, 1, 0), jnp.moveaxis(chunk_decay, 1, 0)))
    prev = jnp.moveaxis(prev, 0, 1)
    y_off = jnp.einsum('bcign,bcgrpn->bcigrp', Cq, prev) * jnp.exp(acs)[..., None]
    return (y_diag + y_off).reshape(b, L, G, R, P)


def encoder_layer(x, g_mix, w_in, b_gate, q_norm_g, k_norm_g, rel_bias, ssd_conv_w, ssd_conv_b,
                  dt_bias_f, dt_bias_b, A_log_f, A_log_b, D_skip, ssd_norm_g, w_att_out, w_ssd_out,
                  w_o, g_ffn, w_up, ffn_conv_w, ffn_conv_b, w_down):
    b, L, _ = x.shape
    f32 = jnp.float32
    h = rmsnorm(x, g_mix)
    q, k, v, z, xbc, dtf, dtb, ga, gs = split_cols(h @ w_in, IN_SIZES)

    q = rmsnorm(q.reshape(b, L, ATT_HEADS, HEAD_DIM), q_norm_g) * (HEAD_DIM ** -0.5)
    k = rmsnorm(k.reshape(b, L, ATT_HEADS, HEAD_DIM), k_norm_g)
    v = v.reshape(b, L, ATT_HEADS, HEAD_DIM)
    att = neighborhood_attention_2d(q, k, v, rel_bias)

    xbc = jax.nn.silu(dwconv_centered(xbc, ssd_conv_w, ssd_conv_b))
    xs, Bm, Cm = split_cols(xbc, (SSD_INNER, SSD_GROUPS * SSD_STATE, SSD_GROUPS * SSD_STATE))
    xs = xs.reshape(b, L, SSD_GROUPS, SSD_HPG, SSD_HEAD_DIM).astype(f32)
    Bm = Bm.reshape(b, L, SSD_GROUPS, SSD_STATE).astype(f32)
    Cm = Cm.reshape(b, L, SSD_GROUPS, SSD_STATE).astype(f32)
    dt_f = jax.nn.softplus(dtf.astype(f32) + dt_bias_f.astype(f32)).reshape(b, L, SSD_GROUPS, SSD_HPG)
    dt_b = jax.nn.softplus(dtb.astype(f32) + dt_bias_b.astype(f32)).reshape(b, L, SSD_GROUPS, SSD_HPG)
    A_f = -jnp.exp(A_log_f.astype(f32)).reshape(SSD_GROUPS, SSD_HPG)
    A_b = -jnp.exp(A_log_b.astype(f32)).reshape(SSD_GROUPS, SSD_HPG)
    rev = lambda t: jnp.flip(t, axis=1)
    y_f = ssd_chunked(xs, dt_f, A_f, Bm, Cm)
    y_b = rev(ssd_chunked(rev(xs), rev(dt_b), A_b, rev(Bm), rev(Cm)))
    y = y_f + y_b + xs * D_skip.astype(f32).reshape(SSD_GROUPS, SSD_HPG, 1)
    y = y.reshape(b, L, SSD_INNER) * jax.nn.silu(z.astype(f32))
    yg = y.reshape(b, L, SSD_GROUPS, SSD_INNER // SSD_GROUPS)
    yg = yg * lax.rsqrt(jnp.mean(yg * yg, axis=-1, keepdims=True) + EPS)
    y_ssd = (yg.reshape(b, L, SSD_INNER) * ssd_norm_g.astype(f32)).astype(x.dtype)

    ba, bs = jnp.split(b_gate, 2)
    merged = jax.nn.sigmoid(ga + ba) * (att @ w_att_out) + jax.nn.sigmoid(gs + bs) * (y_ssd @ w_ssd_out)
    x = x + merged @ w_o

    u = dwconv_centered(rmsnorm(x, g_ffn) @ w_up, ffn_conv_w, ffn_conv_b)
    a, g = jnp.split(u, 2, axis=-1)
    return x + (a * jax.nn.silu(g)) @ w_down


def encoder_trunk(x, params):
    for layer in range(DEPTH):
        x = encoder_layer(x, *[p[layer] for p in params])
    return x


def setup_inputs(seed: int = 0) -> dict:
    key = jax.random.key(seed)
    ks = jax.random.split(key, 26)
    nrm = lambda k, shape, s: jax.random.normal(k, shape, jnp.float32) * s

    def dt_bias(k):
        dt = jnp.exp(jax.random.uniform(k, (DEPTH, SSD_HEADS), jnp.float32, math.log(1e-3), math.log(1e-1)))
        return dt + jnp.log(-jnp.expm1(-dt))

    return {
        "x_prompt": nrm(ks[0], (BATCH, SEQ, D_MODEL), 1.0),
        "x_sample": nrm(ks[1], (DEC_BATCH, DEC_SEQ, D_MODEL), 1.0),
        "g_mix": 1.0 + nrm(ks[2], (DEPTH, D_MODEL), 0.05),
        "w_in": nrm(ks[3], (DEPTH, D_MODEL, IN_W), D_MODEL ** -0.5),
        "b_gate": nrm(ks[4], (DEPTH, 2 * D_MODEL), 0.1),
        "q_norm_g": 1.0 + nrm(ks[5], (DEPTH, HEAD_DIM), 0.05),
        "k_norm_g": 1.0 + nrm(ks[6], (DEPTH, HEAD_DIM), 0.05),
        "rel_bias": nrm(ks[7], (DEPTH, ATT_HEADS, 2 * WIN_R - 1, 2 * WIN_C - 1), 0.5),
        "ssd_conv_w": nrm(ks[8], (DEPTH, SSD_CONV, CONV_DIM), SSD_CONV ** -0.5),
        "ssd_conv_b": nrm(ks[9], (DEPTH, CONV_DIM), 0.02),
        "dt_bias_f": dt_bias(ks[10]),
        "dt_bias_b": dt_bias(ks[11]),
        "A_log_f": jnp.log(jax.random.uniform(ks[12], (DEPTH, SSD_HEADS), jnp.float32, 1.0, 16.0)),
        "A_log_b": jnp.log(jax.random.uniform(ks[13], (DEPTH, SSD_HEADS), jnp.float32, 1.0, 16.0)),
        "D_skip": 1.0 + nrm(ks[14], (DEPTH, SSD_HEADS), 0.1),
        "ssd_norm_g": 1.0 + nrm(ks[15], (DEPTH, SSD_INNER), 0.05),
        "w_att_out": nrm(ks[16], (DEPTH, ATT_W, D_MODEL), ATT_W ** -0.5),
        "w_ssd_out": nrm(ks[17], (DEPTH, SSD_INNER, D_MODEL), SSD_INNER ** -0.5),
        "w_o": nrm(ks[18], (DEPTH, D_MODEL, D_MODEL), D_MODEL ** -0.5),
        "g_ffn": 1.0 + nrm(ks[19], (DEPTH, D_MODEL), 0.05),
        "w_up": nrm(ks[20], (DEPTH, D_MODEL, 2 * D_FF), D_MODEL ** -0.5),
        "ffn_conv_w": nrm(ks[21], (DEPTH, FFN_CONV, 2 * D_FF), FFN_CONV ** -0.5),
        "ffn_conv_b": nrm(ks[22], (DEPTH, 2 * D_FF), 0.02),
        "w_down": nrm(ks[23], (DEPTH, D_FF, D_MODEL), D_FF ** -0.5),
    }


def reference(x_prompt, x_sample, g_mix, w_in, b_gate, q_norm_g, k_norm_g, rel_bias, ssd_conv_w,
              ssd_conv_b, dt_bias_f, dt_bias_b, A_log_f, A_log_b, D_skip, ssd_norm_g, w_att_out,
              w_ssd_out, w_o, g_ffn, w_up, ffn_conv_w, ffn_conv_b, w_down):
    params = (g_mix, w_in, b_gate, q_norm_g, k_norm_g, rel_bias, ssd_conv_w, ssd_conv_b,
              dt_bias_f, dt_bias_b, A_log_f, A_log_b, D_skip, ssd_norm_g, w_att_out, w_ssd_out,
              w_o, g_ffn, w_up, ffn_conv_w, ffn_conv_b, w_down)
    y_prompt = encoder_trunk(x_prompt, params)
    y_sample = encoder_trunk(x_sample, params)
    return (y_prompt, y_sample)
```

```python
import functools

import jax
import jax.numpy as jnp
from jax import lax
from jax.experimental import pallas as pl
from jax.experimental.pallas import tpu as pltpu

F32 = jnp.float32
BF16 = jnp.bfloat16
HIGHEST = lax.Precision.HIGHEST

D_MODEL = 1024
GRID_W = 64
WIN_R = 8
WIN_C = 16
ATT_HEADS = 16
HEAD_DIM = 64
ATT_W = ATT_HEADS * HEAD_DIM
SSD_INNER = 2 * D_MODEL
SSD_HEAD_DIM = 64
SSD_HEADS = SSD_INNER // SSD_HEAD_DIM
SSD_GROUPS = 8
SSD_HPG = SSD_HEADS // SSD_GROUPS
SSD_STATE = 128
SSD_CONV = 5
CONV_DIM = SSD_INNER + 2 * SSD_GROUPS * SSD_STATE
CHUNK = 128
D_FF = 2816
FFN_CONV = 3
EPS = 1e-6

OFF_Q = 0
OFF_K = OFF_Q + ATT_W
OFF_V = OFF_K + ATT_W
OFF_Z = OFF_V + ATT_W
OFF_XBC = OFF_Z + SSD_INNER
OFF_GA = OFF_XBC + CONV_DIM
OFF_GS = OFF_GA + D_MODEL
PROJ_W = OFF_GS + D_MODEL
DT_COL = 3 * ATT_W + SSD_INNER + CONV_DIM

NEG = -1e30
VMEM_LIMIT = 56 * 1024 * 1024


def _cparams(n_axes):
    return pltpu.CompilerParams(
        dimension_semantics=("arbitrary",) * n_axes, vmem_limit_bytes=VMEM_LIMIT)


def _rms(x, g):
    return x * lax.rsqrt(jnp.mean(x * x, axis=-1, keepdims=True) + EPS) * g


def _silu(x):
    return x * jax.nn.sigmoid(x)


PROJ_TM = 1024
PROJ_TN = 1024


def _proj_kernel(x_ref, g_ref, w_ref, wdt_ref, dtb_ref, o_ref, dt_ref, h_ref):
    @pl.when(pl.program_id(1) == 0)
    def _():
        h = _rms(x_ref[...], g_ref[...]).astype(BF16)
        h_ref[...] = h
        raw = jnp.dot(h, wdt_ref[...], preferred_element_type=F32)
        dt_ref[...] = jax.nn.softplus(raw + dtb_ref[...])

    o_ref[...] = jnp.dot(h_ref[...], w_ref[...], preferred_element_type=F32).astype(BF16)


def _proj(x2, g_mix, w_main, w_dt, dt_bias):
    m = x2.shape[0]
    tm = min(PROJ_TM, m)
    return pl.pallas_call(
        _proj_kernel,
        grid=(m // tm, PROJ_W // PROJ_TN),
        in_specs=[
            pl.BlockSpec((tm, D_MODEL), lambda i, j: (i, 0)),
            pl.BlockSpec((1, D_MODEL), lambda i, j: (0, 0)),
            pl.BlockSpec((D_MODEL, PROJ_TN), lambda i, j: (0, j)),
            pl.BlockSpec((D_MODEL, 2 * SSD_HEADS), lambda i, j: (0, 0)),
            pl.BlockSpec((1, 2 * SSD_HEADS), lambda i, j: (0, 0)),
        ],
        out_specs=[
            pl.BlockSpec((tm, PROJ_TN), lambda i, j: (i, j)),
            pl.BlockSpec((tm, 2 * SSD_HEADS), lambda i, j: (i, 0)),
        ],
        out_shape=[
            jax.ShapeDtypeStruct((m, PROJ_W), BF16),
            jax.ShapeDtypeStruct((m, 2 * SSD_HEADS), F32),
        ],
        scratch_shapes=[pltpu.VMEM((tm, D_MODEL), BF16)],
        compiler_params=_cparams(2),
        name="proj",
    )(x2, g_mix, w_main, w_dt, dt_bias)


N_ROFF = 2 * WIN_R - 1
N_COFF = 2 * WIN_C - 1
KEYS = WIN_R * GRID_W


def _bias_kernel(rb_ref, o_ref):
    h = pl.program_id(0)
    d = pl.program_id(1)
    c = lax.broadcasted_iota(jnp.int32, (GRID_W, GRID_W), 0)
    cp = lax.broadcasted_iota(jnp.int32, (GRID_W, GRID_W), 1)
    cs = jnp.clip(c - WIN_C // 2, 0, GRID_W - WIN_C)
    inside = jnp.where(cp >= cs, jnp.where(cp < cs + WIN_C, 1, 0), 0)
    idx = jnp.clip(cp - c + WIN_C - 1, 0, N_COFF - 1)
    for i in range(WIN_R):
        base = (h * N_ROFF + (i - d + WIN_R - 1)) * N_COFF

        def body(k, acc, base=base):
            return jnp.where(idx == k, rb_ref[base + k], acc)

        t = lax.fori_loop(0, N_COFF, body, jnp.zeros((GRID_W, GRID_W), F32))
        o_ref[0, 0, :, i * GRID_W:(i + 1) * GRID_W] = jnp.where(inside == 1, t, NEG)


def _bias_table(rel_bias):
    return pl.pallas_call(
        _bias_kernel,
        grid=(ATT_HEADS, WIN_R),
        in_specs=[pl.BlockSpec(memory_space=pltpu.SMEM)],
        out_specs=pl.BlockSpec((1, 1, GRID_W, KEYS), lambda h, d: (h, d, 0, 0)),
        out_shape=jax.ShapeDtypeStruct((ATT_HEADS, WIN_R, GRID_W, KEYS), F32),
        compiler_params=_cparams(2),
        name="bias_table",
    )(rel_bias.reshape(-1))


NORM_ROWS = 256


def _attn_kernel(q_ref, k_ref, v_ref, gq_ref, gk_ref, bias_ref, o_ref, qn_ref, kn_ref, *, rows):
    lane = lax.broadcasted_iota(jnp.int32, (2 * HEAD_DIM, 2 * HEAD_DIM), 0) // HEAD_DIM
    lane_t = lax.broadcasted_iota(jnp.int32, (2 * HEAD_DIM, 2 * HEAD_DIM), 1) // HEAD_DIM
    seg = jnp.where(lane == lane_t, 1.0 / HEAD_DIM, 0.0).astype(F32)

    def head_norm(x, g):
        ms = jnp.dot(x * x, seg, precision=HIGHEST, preferred_element_type=F32)
        return x * lax.rsqrt(ms + EPS) * g

    def norm_body(i, carry):
        sl = pl.ds(pl.multiple_of(i * NORM_ROWS, NORM_ROWS), NORM_ROWS)
        qn = head_norm(q_ref[sl, :].astype(F32), gq_ref[...]) * (HEAD_DIM ** -0.5)
        qn_ref[sl, :] = qn.astype(BF16)
        kn_ref[sl, :] = head_norm(k_ref[sl, :].astype(F32), gk_ref[...]).astype(BF16)
        return carry

    lax.fori_loop(0, (rows * GRID_W) // NORM_ROWS, norm_body, 0)

    def row_body(r, carry):
        rs = jnp.clip(r - WIN_R // 2, 0, rows - WIN_R)
        d = r - rs
        qrow = pl.ds(pl.multiple_of(r * GRID_W, GRID_W), GRID_W)
        krow = pl.ds(pl.multiple_of(rs * GRID_W, GRID_W), KEYS)
        q2 = qn_ref[qrow, :]
        k2 = kn_ref[krow, :]
        v2 = v_ref[krow, :]
        outs = []
        for h in range(2):
            hs = slice(h * HEAD_DIM, (h + 1) * HEAD_DIM)
            s = lax.dot_general(q2[:, hs], k2[:, hs], (((1,), (1,)), ((), ())),
                                preferred_element_type=F32)
            s = s + bias_ref[h, d]
            p = jnp.exp(s - jnp.max(s, axis=-1, keepdims=True))
            den = jnp.sum(p, axis=-1, keepdims=True)
            o = jnp.dot(p.astype(BF16), v2[:, hs], preferred_element_type=F32)
            outs.append(o / den)
        o_ref[qrow, :] = jnp.concatenate(outs, axis=-1).astype(BF16)
        return carry

    lax.fori_loop(0, rows, row_body, 0)


def _attention(packed3, q_norm_g, k_norm_g, bias_tab):
    b, l, _ = packed3.shape
    rows = l // GRID_W
    pair = 2 * HEAD_DIM
    gq = jnp.tile(q_norm_g.reshape(1, HEAD_DIM), (1, 2))
    gk = jnp.tile(k_norm_g.reshape(1, HEAD_DIM), (1, 2))
    col = lambda off: (lambda i, j: (i, 0, off // pair + j))
    return pl.pallas_call(
        functools.partial(_attn_kernel, rows=rows),
        grid=(b, ATT_HEADS // 2),
        in_specs=[
            pl.BlockSpec((None, l, pair), col(OFF_Q)),
            pl.BlockSpec((None, l, pair), col(OFF_K)),
            pl.BlockSpec((None, l, pair), col(OFF_V)),
            pl.BlockSpec((1, pair), lambda i, j: (0, 0)),
            pl.BlockSpec((1, pair), lambda i, j: (0, 0)),
            pl.BlockSpec((2, WIN_R, GRID_W, KEYS), lambda i, j: (j, 0, 0, 0)),
        ],
        out_specs=pl.BlockSpec((None, l, pair), lambda i, j: (i, 0, j)),
        out_shape=jax.ShapeDtypeStruct((b, l, ATT_W), BF16),
        scratch_shapes=[pltpu.VMEM((l, pair), BF16), pltpu.VMEM((l, pair), BF16)],
        compiler_params=_cparams(2),
        name="attention",
    )(packed3, packed3, packed3, gq, gk, bias_tab)


CONV_TN = 1024
HALO = 16


def _roll_rows(x, offset, lo, n):
    if offset == 0:
        return x[lo:lo + n]
    return pltpu.roll(x, (-offset) % x.shape[0], 0)[lo:lo + n]


def _conv_kernel(prev_ref, main_ref, next_ref, w_ref, b_ref, o_ref):
    c = pl.program_id(1)
    nc = pl.num_programs(1)
    prev = jnp.where(c > 0, prev_ref[...].astype(F32)[HALO - 8:], 0.0)
    nxt = jnp.where(c < nc - 1, next_ref[...].astype(F32)[:8], 0.0)
    ext = jnp.concatenate([prev, main_ref[...].astype(F32), nxt], axis=0)
    acc = jnp.broadcast_to(b_ref[...], (CHUNK, CONV_TN))
    for k in range(SSD_CONV):
        acc = acc + _roll_rows(ext, k - SSD_CONV // 2, 8, CHUNK) * w_ref[k:k + 1, :]
    o_ref[...] = _silu(acc).astype(BF16)


def _ssd_conv(packed3, conv_w, conv_b):
    b, l, _ = packed3.shape
    nc = l // CHUNK
    hb = CHUNK // HALO
    c0 = OFF_XBC // CONV_TN
    return pl.pallas_call(
        _conv_kernel,
        grid=(b, nc, CONV_DIM // CONV_TN),
        in_specs=[
            pl.BlockSpec((None, HALO, CONV_TN), lambda i, c, j: (i, jnp.maximum(c * hb - 1, 0), c0 + j)),
            pl.BlockSpec((None, CHUNK, CONV_TN), lambda i, c, j: (i, c, c0 + j)),
            pl.BlockSpec((None, HALO, CONV_TN),
                         lambda i, c, j: (i, jnp.minimum((c + 1) * hb, l // HALO - 1), c0 + j)),
            pl.BlockSpec((SSD_CONV, CONV_TN), lambda i, c, j: (0, j)),
            pl.BlockSpec((1, CONV_TN), lambda i, c, j: (0, j)),
        ],
        out_specs=pl.BlockSpec((None, CHUNK, CONV_TN), lambda i, c, j: (i, c, j)),
        out_shape=jax.ShapeDtypeStruct((b, l, CONV_DIM), BF16),
        compiler_params=_cparams(3),
        name="ssd_conv",
    )(packed3, packed3, packed3, conv_w, conv_b)


GW = SSD_HPG * SSD_HEAD_DIM


def _ssd_kernel(x_ref, b_ref, c_ref, z_ref, dt_ref, alog_ref, dskip_ref, ng_ref, o_ref,
                yf_ref, st_ref, *, nc):
    g = pl.program_id(1)
    t = pl.program_id(2)
    fwd = t < nc
    chunk = jnp.where(fwd, t, 2 * nc - 1 - t)
    sgn = jnp.where(fwd, 1, -1)
    head0 = jnp.where(fwd, 0, SSD_HEADS) + g * SSD_HPG

    @pl.when((t == 0) | (t == nc))
    def _():
        st_ref[...] = jnp.zeros_like(st_ref)

    nh = 2 * SSD_HEADS
    r64 = lax.broadcasted_iota(jnp.int32, (nh, GW), 0)
    l64 = lax.broadcasted_iota(jnp.int32, (nh, GW), 1) // SSD_HEAD_DIM
    e_head = jnp.where(r64 == head0 + l64, 1.0, 0.0).astype(F32)
    r128 = lax.broadcasted_iota(jnp.int32, (nh, SSD_HPG * CHUNK), 0)
    l128 = lax.broadcasted_iota(jnp.int32, (nh, SSD_HPG * CHUNK), 1) // CHUNK
    e_wide = jnp.where(r128 == head0 + l128, 1.0, 0.0).astype(F32)
    rt = lax.broadcasted_iota(jnp.int32, (8, nh), 0)
    ct = lax.broadcasted_iota(jnp.int32, (8, nh), 1)
    e_rows = jnp.where(ct == head0 + rt, 1.0, 0.0).astype(F32)

    ii = lax.broadcasted_iota(jnp.int32, (CHUNK, CHUNK), 0)
    jj = lax.broadcasted_iota(jnp.int32, (CHUNK, CHUNK), 1)
    keep = sgn * (jj - ii) <= 0
    tri = jnp.where(keep, 1.0, 0.0).astype(F32)
    tri_t = jnp.where(sgn * (ii - jj) <= 0, 1.0, 0.0).astype(F32)

    hdot = functools.partial(jnp.dot, precision=HIGHEST, preferred_element_type=F32)
    dt = dt_ref[...]
    a = dt * (-jnp.exp(alog_ref[...]))
    dt_e = hdot(dt, e_head)
    acs_e = hdot(tri, hdot(a, e_head))
    acs_w = hdot(tri, hdot(a, e_wide))
    a_rows = lax.dot_general(e_rows, a, (((1,), (1,)), ((), ())), precision=HIGHEST,
                             preferred_element_type=F32)
    acs_r = hdot(a_rows, tri_t)

    xb = x_ref[...]
    xf = xb.astype(F32)
    bm = b_ref[...]
    cm = c_ref[...]
    xdt = xf * dt_e
    xdt_b = xdt.astype(BF16)
    scores = lax.dot_general(cm, bm, (((1,), (1,)), ((), ())), preferred_element_type=F32)

    parts = []
    for r in range(SSD_HPG):
        seg = acs_w[:, r * CHUNK:(r + 1) * CHUNK] - acs_r[r:r + 1, :]
        gmat = scores * jnp.exp(jnp.where(keep, seg, NEG))
        parts.append(jnp.dot(gmat.astype(BF16), xdt_b[:, r * SSD_HEAD_DIM:(r + 1) * SSD_HEAD_DIM],
                             preferred_element_type=F32))
    y = jnp.concatenate(parts, axis=-1)

    st = st_ref[...]
    y = y + jnp.dot(cm, st.astype(BF16), preferred_element_type=F32) * jnp.exp(acs_e)

    total = jnp.where(fwd, acs_e[CHUNK - 1:CHUNK, :], acs_e[0:1, :])
    xw = (xdt * jnp.exp(total - acs_e)).astype(BF16)
    st_ref[...] = st * jnp.exp(total) + lax.dot_general(
        bm, xw, (((0,), (0,)), ((), ())), preferred_element_type=F32)

    rows = pl.ds(pl.multiple_of(chunk * CHUNK, CHUNK), CHUNK)

    @pl.when(fwd)
    def _():
        yf_ref[rows, :] = y

    @pl.when(jnp.logical_not(fwd))
    def _():
        yt = (yf_ref[rows, :] + y + xf * dskip_ref[...]) * _silu(z_ref[...].astype(F32))
        o_ref[...] = _rms(yt, ng_ref[...]).astype(BF16)


def _ssd(xbc3, packed3, dt3, a_log, d_skip_e, norm_g):
    b, l, _ = xbc3.shape
    nc = l // CHUNK
    ch = lambda t: jnp.where(t < nc, t, 2 * nc - 1 - t)
    ch_out = lambda t: jnp.where(t < nc, nc - 1, 2 * nc - 1 - t)
    b0 = SSD_INNER // SSD_STATE
    return pl.pallas_call(
        functools.partial(_ssd_kernel, nc=nc),
        grid=(b, SSD_GROUPS, 2 * nc),
        in_specs=[
            pl.BlockSpec((None, CHUNK, GW), lambda i, g, t: (i, ch(t), g)),
            pl.BlockSpec((None, CHUNK, SSD_STATE), lambda i, g, t: (i, ch(t), b0 + g)),
            pl.BlockSpec((None, CHUNK, SSD_STATE), lambda i, g, t: (i, ch(t), b0 + SSD_GROUPS + g)),
            pl.BlockSpec((None, CHUNK, GW), lambda i, g, t: (i, ch(t), OFF_Z // GW + g)),
            pl.BlockSpec((None, CHUNK, 2 * SSD_HEADS), lambda i, g, t: (i, ch(t), 0)),
            pl.BlockSpec((1, 2 * SSD_HEADS), lambda i, g, t: (0, 0)),
            pl.BlockSpec((1, GW), lambda i, g, t: (0, g)),
            pl.BlockSpec((1, GW), lambda i, g, t: (0, g)),
        ],
        out_specs=pl.BlockSpec((None, CHUNK, GW), lambda i, g, t: (i, ch_out(t), g)),
        out_shape=jax.ShapeDtypeStruct((b, l, SSD_INNER), BF16),
        scratch_shapes=[pltpu.VMEM((l, GW), F32), pltpu.VMEM((SSD_STATE, GW), F32)],
        compiler_params=_cparams(3),
        name="ssd",
    )(xbc3, xbc3, xbc3, packed3, dt3, a_log, d_skip_e, norm_g)


MERGE_TM = 512


def _merge_kernel(att_ref, y_ref, ga_ref, gs_ref, x_ref, ba_ref, bs_ref, wa_ref, ws_ref, wo_ref, o_ref):
    m1 = jnp.dot(att_ref[...], wa_ref[...], preferred_element_type=F32)
    m2 = jnp.dot(y_ref[...], ws_ref[...], preferred_element_type=F32)
    merged = (jax.nn.sigmoid(ga_ref[...].astype(F32) + ba_ref[...]) * m1
              + jax.nn.sigmoid(gs_ref[...].astype(F32) + bs_ref[...]) * m2)
    o_ref[...] = x_ref[...] + jnp.dot(merged.astype(BF16), wo_ref[...], preferred_element_type=F32)


def _merge(att2, y2, packed2, x2, b_gate, wa, ws, wo):
    m = x2.shape[0]
    tm = min(MERGE_TM, m)
    row = lambda w, j=0: pl.BlockSpec((tm, w), lambda i: (i, j))
    full = lambda r, c, j=0: pl.BlockSpec((r, c), lambda i: (0, j))
    return pl.pallas_call(
        _merge_kernel,
        grid=(m // tm,),
        in_specs=[
            row(ATT_W), row(SSD_INNER), row(D_MODEL, OFF_GA // D_MODEL), row(D_MODEL, OFF_GS // D_MODEL),
            row(D_MODEL), full(1, D_MODEL, 0), full(1, D_MODEL, 1),
            full(ATT_W, D_MODEL), full(SSD_INNER, D_MODEL), full(D_MODEL, D_MODEL),
        ],
        out_specs=row(D_MODEL),
        out_shape=jax.ShapeDtypeStruct((m, D_MODEL), F32),
        compiler_params=_cparams(1),
        name="merge",
    )(att2, y2, packed2, packed2, x2, b_gate, b_gate, wa, ws, wo)


FFN_TM = 512
FFN_TF = 1408
FFN_HALO = 8


def _ffn_kernel(prev_ref, x_ref, next_ref, g_ref, wa_ref, wg_ref, cwa_ref, cwg_ref, cba_ref, cbg_ref,
                wd_ref, o_ref, h_ref, acc_ref):
    m = pl.program_id(1)
    f = pl.program_id(2)
    tm = x_ref.shape[0]

    @pl.when(f == 0)
    def _():
        prev = jnp.where(m > 0, prev_ref[...], 0.0)
        nxt = jnp.where(m < pl.num_programs(1) - 1, next_ref[...], 0.0)
        ext = jnp.concatenate([prev, x_ref[...], nxt], axis=0)
        h_ref[...] = _rms(ext, g_ref[...]).astype(BF16)
        acc_ref[...] = jnp.zeros_like(acc_ref)

    h = h_ref[...]

    def conv(w_ref, cw_ref, cb_ref):
        u = jnp.dot(h, w_ref[...], preferred_element_type=F32)
        out = jnp.broadcast_to(cb_ref[...], (tm, u.shape[1]))
        for k in range(FFN_CONV):
            out = out + _roll_rows(u, k - FFN_CONV // 2, FFN_HALO, tm) * cw_ref[k:k + 1, :]
        return out

    act = conv(wa_ref, cwa_ref, cba_ref) * _silu(conv(wg_ref, cwg_ref, cbg_ref))
    acc_ref[...] += jnp.dot(act.astype(BF16), wd_ref[...], preferred_element_type=F32)

    @pl.when(f == pl.num_programs(2) - 1)
    def _():
        o_ref[...] = x_ref[...] + acc_ref[...]


def _ffn(x3, g_ffn, w_up, conv_w, conv_b, w_down):
    b, l, _ = x3.shape
    tm = min(FFN_TM, l)
    nf = D_FF // FFN_TF
    hb = tm // FFN_HALO
    wcol = lambda r, j0: pl.BlockSpec((r, FFN_TF), lambda i, m, f: (0, j0 + f))
    return pl.pallas_call(
        _ffn_kernel,
        grid=(b, l // tm, nf),
        in_specs=[
            pl.BlockSpec((None, FFN_HALO, D_MODEL), lambda i, m, f: (i, jnp.maximum(m * hb - 1, 0), 0)),
            pl.BlockSpec((None, tm, D_MODEL), lambda i, m, f: (i, m, 0)),
            pl.BlockSpec((None, FFN_HALO, D_MODEL),
                         lambda i, m, f: (i, jnp.minimum((m + 1) * hb, l // FFN_HALO - 1), 0)),
            pl.BlockSpec((1, D_MODEL), lambda i, m, f: (0, 0)),
            wcol(D_MODEL, 0), wcol(D_MODEL, nf),
            wcol(FFN_CONV, 0), wcol(FFN_CONV, nf),
            wcol(1, 0), wcol(1, nf),
            pl.BlockSpec((FFN_TF, D_MODEL), lambda i, m, f: (f, 0)),
        ],
        out_specs=pl.BlockSpec((None, tm, D_MODEL), lambda i, m, f: (i, m, 0)),
        out_shape=jax.ShapeDtypeStruct((b, l, D_MODEL), F32),
        scratch_shapes=[pltpu.VMEM((tm + 2 * FFN_HALO, D_MODEL), BF16), pltpu.VMEM((tm, D_MODEL), F32)],
        compiler_params=_cparams(3),
        name="ffn",
    )(x3, x3, x3, g_ffn, w_up, w_up, conv_w, conv_w, conv_b, conv_b, w_down)


def _layer(x, bias_tab, p):
    b, l, _ = x.shape
    x2 = x.reshape(b * l, D_MODEL)
    packed2, dt2 = _proj(x2, p["g_mix"], p["w_main"], p["w_dt"], p["dt_bias"])
    packed3 = packed2.reshape(b, l, PROJ_W)
    att = _attention(packed3, p["q_norm_g"], p["k_norm_g"], bias_tab)
    xbc = _ssd_conv(packed3, p["ssd_conv_w"], p["ssd_conv_b"])
    y = _ssd(xbc, packed3, dt2.reshape(b, l, 2 * SSD_HEADS), p["a_log"], p["d_skip_e"], p["ssd_norm_g"])
    x1 = _merge(att.reshape(b * l, ATT_W), y.reshape(b * l, SSD_INNER), packed2, x2,
                p["b_gate"], p["w_att_out"], p["w_ssd_out"], p["w_o"])
    return _ffn(x1.reshape(b, l, D_MODEL), p["g_ffn"], p["w_up"], p["ffn_conv_w"], p["ffn_conv_b"],
                p["w_down"])


def kernel(x_prompt, x_sample, g_mix, w_in, b_gate, q_norm_g, k_norm_g, rel_bias, ssd_conv_w, ssd_conv_b,
           dt_bias_f, dt_bias_b, A_log_f, A_log_b, D_skip, ssd_norm_g, w_att_out, w_ssd_out, w_o, g_ffn,
           w_up, ffn_conv_w, ffn_conv_b, w_down):
    depth = g_mix.shape[0]
    xs = [x_prompt, x_sample]
    for layer in range(depth):
        w = w_in[layer]
        p = {
            "g_mix": g_mix[layer][None, :],
            "w_main": jnp.concatenate([w[:, :DT_COL], w[:, DT_COL + 2 * SSD_HEADS:]], axis=1).astype(BF16),
            "w_dt": w[:, DT_COL:DT_COL + 2 * SSD_HEADS].astype(BF16),
            "dt_bias": jnp.concatenate([dt_bias_f[layer], dt_bias_b[layer]])[None, :],
            "q_norm_g": q_norm_g[layer],
            "k_norm_g": k_norm_g[layer],
            "ssd_conv_w": ssd_conv_w[layer],
            "ssd_conv_b": ssd_conv_b[layer][None, :],
            "a_log": jnp.concatenate([A_log_f[layer], A_log_b[layer]])[None, :],
            "d_skip_e": jnp.repeat(D_skip[layer], SSD_HEAD_DIM)[None, :],
            "ssd_norm_g": ssd_norm_g[layer][None, :],
            "b_gate": b_gate[layer][None, :],
            "w_att_out": w_att_out[layer].astype(BF16),
            "w_ssd_out": w_ssd_out[layer].astype(BF16),
            "w_o": w_o[layer].astype(BF16),
            "g_ffn": g_ffn[layer][None, :],
            "w_up": w_up[layer].astype(BF16),
            "ffn_conv_w": ffn_conv_w[layer],
            "ffn_conv_b": ffn_conv_b[layer][None, :],
            "w_down": w_down[layer].astype(BF16),
        }
        bias_tab = _bias_table(rel_bias[layer])
        xs = [_layer(x, bias_tab, p) for x in xs]
    return tuple(xs)
```

```python
import functools

import jax
import jax.numpy as jnp
from jax import lax
from jax.experimental import pallas as pl
from jax.experimental.pallas import tpu as pltpu

F32 = jnp.float32
BF16 = jnp.bfloat16
HIGHEST = lax.Precision.HIGHEST

D_MODEL = 1024
GRID_W = 64
WIN_R = 8
WIN_C = 16
ATT_HEADS = 16
HEAD_DIM = 64
ATT_W = ATT_HEADS * HEAD_DIM
SSD_INNER = 2 * D_MODEL
SSD_HEAD_DIM = 64
SSD_HEADS = SSD_INNER // SSD_HEAD_DIM
SSD_GROUPS = 8
SSD_HPG = SSD_HEADS // SSD_GROUPS
SSD_STATE = 128
SSD_CONV = 5
CONV_DIM = SSD_INNER + 2 * SSD_GROUPS * SSD_STATE
CHUNK = 128
D_FF = 2816
FFN_CONV = 3
EPS = 1e-6

OFF_Q = 0
OFF_K = OFF_Q + ATT_W
OFF_V = OFF_K + ATT_W
OFF_Z = OFF_V + ATT_W
OFF_XBC = OFF_Z + SSD_INNER
OFF_GA = OFF_XBC + CONV_DIM
OFF_GS = OFF_GA + D_MODEL
PROJ_W = OFF_GS + D_MODEL
DT_COL = 3 * ATT_W + SSD_INNER + CONV_DIM

NEG = -1e30
VMEM_LIMIT = 56 * 1024 * 1024


def _cparams(n_axes):
    return pltpu.CompilerParams(
        dimension_semantics=("arbitrary",) * n_axes, vmem_limit_bytes=VMEM_LIMIT)


def _rms(x, g):
    return x * lax.rsqrt(jnp.mean(x * x, axis=-1, keepdims=True) + EPS) * g


def _silu(x):
    return x * jax.nn.sigmoid(x)


PROJ_TM = 1024
PROJ_TN = 1024


def _proj_kernel(x_ref, g_ref, w_ref, wdt_ref, dtb_ref, o_ref, dt_ref, h_ref):
    @pl.when(pl.program_id(1) == 0)
    def _():
        h = _rms(x_ref[...], g_ref[...]).astype(BF16)
        h_ref[...] = h
        raw = jnp.dot(h, wdt_ref[...], preferred_element_type=F32)
        dt_ref[...] = jax.nn.softplus(raw + dtb_ref[...])

    o_ref[...] = jnp.dot(h_ref[...], w_ref[...], preferred_element_type=F32).astype(BF16)


def _proj(x2, g_mix, w_main, w_dt, dt_bias):
    m = x2.shape[0]
    tm = min(PROJ_TM, m)
    return pl.pallas_call(
        _proj_kernel,
        grid=(m // tm, PROJ_W // PROJ_TN),
        in_specs=[
            pl.BlockSpec((tm, D_MODEL), lambda i, j: (i, 0)),
            pl.BlockSpec((1, D_MODEL), lambda i, j: (0, 0)),
            pl.BlockSpec((D_MODEL, PROJ_TN), lambda i, j: (0, j)),
            pl.BlockSpec((D_MODEL, 2 * SSD_HEADS), lambda i, j: (0, 0)),
            pl.BlockSpec((1, 2 * SSD_HEADS), lambda i, j: (0, 0)),
        ],
        out_specs=[
            pl.BlockSpec((tm, PROJ_TN), lambda i, j: (i, j)),
            pl.BlockSpec((tm, 2 * SSD_HEADS), lambda i, j: (i, 0)),
        ],
        out_shape=[
            jax.ShapeDtypeStruct((m, PROJ_W), BF16),
            jax.ShapeDtypeStruct((m, 2 * SSD_HEADS), F32),
        ],
        scratch_shapes=[pltpu.VMEM((tm, D_MODEL), BF16)],
        compiler_params=_cparams(2),
        name="proj",
    )(x2, g_mix, w_main, w_dt, dt_bias)


N_ROFF = 2 * WIN_R - 1
N_COFF = 2 * WIN_C - 1
KEYS = WIN_R * GRID_W


def _bias_kernel(rb_ref, o_ref):
    h = pl.program_id(0)
    d = pl.program_id(1)
    c = lax.broadcasted_iota(jnp.int32, (GRID_W, GRID_W), 0)
    cp = lax.broadcasted_iota(jnp.int32, (GRID_W, GRID_W), 1)
    cs = jnp.clip(c - WIN_C // 2, 0, GRID_W - WIN_C)
    inside = jnp.where(cp >= cs, jnp.where(cp < cs + WIN_C, 1, 0), 0)
    idx = jnp.clip(cp - c + WIN_C - 1, 0, N_COFF - 1)
    for i in range(WIN_R):
        base = (h * N_ROFF + (i - d + WIN_R - 1)) * N_COFF

        def body(k, acc, base=base):
            return jnp.where(idx == k, rb_ref[base + k], acc)

        t = lax.fori_loop(0, N_COFF, body, jnp.zeros((GRID_W, GRID_W), F32))
        o_ref[0, 0, :, i * GRID_W:(i + 1) * GRID_W] = jnp.where(inside == 1, t, NEG)


def _bias_table(rel_bias):
    return pl.pallas_call(
        _bias_kernel,
        grid=(ATT_HEADS, WIN_R),
        in_specs=[pl.BlockSpec(memory_space=pltpu.SMEM)],
        out_specs=pl.BlockSpec((1, 1, GRID_W, KEYS), lambda h, d: (h, d, 0, 0)),
        out_shape=jax.ShapeDtypeStruct((ATT_HEADS, WIN_R, GRID_W, KEYS), F32),
        compiler_params=_cparams(2),
        name="bias_table",
    )(rel_bias.reshape(-1))


NORM_ROWS = 256


def _attn_kernel(q_ref, k_ref, v_ref, gq_ref, gk_ref, bias_ref, o_ref, qn_ref, kn_ref, *, rows):
    lane = lax.broadcasted_iota(jnp.int32, (2 * HEAD_DIM, 2 * HEAD_DIM), 0) // HEAD_DIM
    lane_t = lax.broadcasted_iota(jnp.int32, (2 * HEAD_DIM, 2 * HEAD_DIM), 1) // HEAD_DIM
    seg = jnp.where(lane == lane_t, 1.0 / HEAD_DIM, 0.0).astype(F32)

    def head_norm(x, g):
        ms = jnp.dot(x * x, seg, precision=HIGHEST, preferred_element_type=F32)
        return x * lax.rsqrt(ms + EPS) * g

    def norm_body(i, carry):
        sl = pl.ds(pl.multiple_of(i * NORM_ROWS, NORM_ROWS), NORM_ROWS)
        qn = head_norm(q_ref[sl, :].astype(F32), gq_ref[...]) * (HEAD_DIM ** -0.5)
        qn_ref[sl, :] = qn.astype(BF16)
        kn_ref[sl, :] = head_norm(k_ref[sl, :].astype(F32), gk_ref[...]).astype(BF16)
        return carry

    lax.fori_loop(0, (rows * GRID_W) // NORM_ROWS, norm_body, 0)

    def row_body(r, carry):
        rs = jnp.clip(r - WIN_R // 2, 0, rows - WIN_R)
        d = r - rs
        qrow = pl.ds(pl.multiple_of(r * GRID_W, GRID_W), GRID_W)
        krow = pl.ds(pl.multiple_of(rs * GRID_W, GRID_W), KEYS)
        q2 = qn_ref[qrow, :]
        k2 = kn_ref[krow, :]
        v2 = v_ref[krow, :]
        outs = []
        for h in range(2):
            hs = slice(h * HEAD_DIM, (h + 1) * HEAD_DIM)
            s = lax.dot_general(q2[:, hs], k2[:, hs], (((1,), (1,)), ((), ())),
                                preferred_element_type=F32)
            s = s + bias_ref[h, d]
            p = jnp.exp(s - jnp.max(s, axis=-1, keepdims=True))
            den = jnp.sum(p, axis=-1, keepdims=True)
            o = jnp.dot(p.astype(BF16), v2[:, hs], preferred_element_type=F32)
            outs.append(o / den)
        o_ref[qrow, :] = jnp.concatenate(outs, axis=-1).astype(BF16)
        return carry

    lax.fori_loop(0, rows, row_body, 0)


def _attention(packed3, q_norm_g, k_norm_g, bias_tab):
    b, l, _ = packed3.shape
    rows = l // GRID_W
    pair = 2 * HEAD_DIM
    gq = jnp.tile(q_norm_g.reshape(1, HEAD_DIM), (1, 2))
    gk = jnp.tile(k_norm_g.reshape(1, HEAD_DIM), (1, 2))
    col = lambda off: (lambda i, j: (i, 0, off // pair + j))
    return pl.pallas_call(
        functools.partial(_attn_kernel, rows=rows),
        grid=(b, ATT_HEADS // 2),
        in_specs=[
            pl.BlockSpec((None, l, pair), col(OFF_Q)),
            pl.BlockSpec((None, l, pair), col(OFF_K)),
            pl.BlockSpec((None, l, pair), col(OFF_V)),
            pl.BlockSpec((1, pair), lambda i, j: (0, 0)),
            pl.BlockSpec((1, pair), lambda i, j: (0, 0)),
            pl.BlockSpec((2, WIN_R, GRID_W, KEYS), lambda i, j: (j, 0, 0, 0)),
        ],
        out_specs=pl.BlockSpec((None, l, pair), lambda i, j: (i, 0, j)),
        out_shape=jax.ShapeDtypeStruct((b, l, ATT_W), BF16),
        scratch_shapes=[pltpu.VMEM((l, pair), BF16), pltpu.VMEM((l, pair), BF16)],
        compiler_params=_cparams(2),
        name="attention",
    )(packed3, packed3, packed3, gq, gk, bias_tab)


CONV_TN = 1024
HALO = 16


def _roll_rows(x, offset, lo, n):
    if offset == 0:
        return x[lo:lo + n]
    return pltpu.roll(x, (-offset) % x.shape[0], 0)[lo:lo + n]


def _conv_kernel(prev_ref, main_ref, next_ref, w_ref, b_ref, o_ref):
    c = pl.program_id(1)
    nc = pl.num_programs(1)
    prev = jnp.where(c > 0, prev_ref[...].astype(F32)[HALO - 8:], 0.0)
    nxt = jnp.where(c < nc - 1, next_ref[...].astype(F32)[:8], 0.0)
    ext = jnp.concatenate([prev, main_ref[...].astype(F32), nxt], axis=0)
    acc = jnp.broadcast_to(b_ref[...], (CHUNK, CONV_TN))
    for k in range(SSD_CONV):
        acc = acc + _roll_rows(ext, k - SSD_CONV // 2, 8, CHUNK) * w_ref[k:k + 1, :]
    o_ref[...] = _silu(acc).astype(BF16)


def _ssd_conv(packed3, conv_w, conv_b):
    b, l, _ = packed3.shape
    nc = l // CHUNK
    hb = CHUNK // HALO
    c0 = OFF_XBC // CONV_TN
    return pl.pallas_call(
        _conv_kernel,
        grid=(b, nc, CONV_DIM // CONV_TN),
        in_specs=[
            pl.BlockSpec((None, HALO, CONV_TN), lambda i, c, j: (i, jnp.maximum(c * hb - 1, 0), c0 + j)),
            pl.BlockSpec((None, CHUNK, CONV_TN), lambda i, c, j: (i, c, c0 + j)),
            pl.BlockSpec((None, HALO, CONV_TN),
                         lambda i, c, j: (i, jnp.minimum((c + 1) * hb, l // HALO - 1), c0 + j)),
            pl.BlockSpec((SSD_CONV, CONV_TN), lambda i, c, j: (0, j)),
            pl.BlockSpec((1, CONV_TN), lambda i, c, j: (0, j)),
        ],
        out_specs=pl.BlockSpec((None, CHUNK, CONV_TN), lambda i, c, j: (i, c, j)),
        out_shape=jax.ShapeDtypeStruct((b, l, CONV_DIM), BF16),
        compiler_params=_cparams(3),
        name="ssd_conv",
    )(packed3, packed3, packed3, conv_w, conv_b)


GW = SSD_HPG * SSD_HEAD_DIM


def _split3(v):
    hi = v.astype(BF16)
    r1 = v - hi.astype(F32)
    mid = r1.astype(BF16)
    lo = (r1 - mid.astype(F32)).astype(BF16)
    return hi, mid, lo


ND = 2 * SSD_HPG
NDT = 2 * SSD_HEADS
PIECE_W = 6 * NDT
SPREAD_W = SSD_HPG * CHUNK + GW


def _group_major_perm():
    return [d * SSD_HEADS + g * SSD_HPG + r for g in range(SSD_GROUPS) for d in range(2) for r in range(SSD_HPG)]


def _piece_layout():
    import numpy as np
    m = np.zeros((PIECE_W, SSD_GROUPS * CHUNK), np.float32)
    for q in range(6):
        for g in range(SSD_GROUPS):
            for k in range(ND):
                m[q * NDT + ND * g + k, CHUNK * g + ND * q + k] = 1.0
    return jnp.asarray(m, BF16)


def _spread_layout():
    import numpy as np
    m = np.zeros((2, CHUNK, SPREAD_W), np.float32)
    for d in range(2):
        for q in range(6):
            for r in range(SSD_HPG):
                row = ND * q + SSD_HPG * d + r
                if q < 3:
                    m[d, row, CHUNK * r:CHUNK * (r + 1)] = 1.0
                else:
                    m[d, row, SSD_HPG * CHUNK + SSD_HEAD_DIM * r:SSD_HPG * CHUNK + SSD_HEAD_DIM * (r + 1)] = 1.0
    return jnp.asarray(m, BF16)


def _prep_kernel(dt_ref, alog_ref, lay_ref, pcs_ref, acsr_ref):
    dt = dt_ref[...]
    a = dt * (-jnp.exp(alog_ref[...]))
    ii = lax.broadcasted_iota(jnp.int32, (CHUNK, CHUNK), 0)
    jj = lax.broadcasted_iota(jnp.int32, (CHUNK, CHUNK), 1)
    tri_f = jnp.where(jj <= ii, 1.0, 0.0).astype(BF16)
    tri_b = jnp.where(jj >= ii, 1.0, 0.0).astype(BF16)
    a3 = jnp.concatenate(_split3(a), axis=-1)
    col = lax.broadcasted_iota(jnp.int32, (1, 3 * NDT), 1)
    is_fwd = (col % ND) < SSD_HPG
    s3 = jnp.where(is_fwd, jnp.dot(tri_f, a3, preferred_element_type=F32),
                   jnp.dot(tri_b, a3, preferred_element_type=F32))
    acs = s3[:, 0:NDT] + s3[:, NDT:2 * NDT] + s3[:, 2 * NDT:3 * NDT]
    acs3 = _split3(acs)
    pieces = jnp.concatenate(acs3 + _split3(dt), axis=-1)
    pcs_ref[...] = jnp.dot(pieces, lay_ref[...], preferred_element_type=F32).astype(BF16)
    er = lax.broadcasted_iota(jnp.int32, (NDT, NDT), 0)
    ec = lax.broadcasted_iota(jnp.int32, (NDT, NDT), 1)
    eye = jnp.where(er == ec, 1.0, 0.0).astype(BF16)
    rows = None
    for p in acs3:
        d = lax.dot_general(eye, p, (((1,), (1,)), ((), ())), preferred_element_type=F32)
        rows = d if rows is None else rows + d
    acsr_ref[...] = rows


def _ssd_prep(dt3, a_log):
    b, l, _ = dt3.shape
    nc = l // CHUNK
    return pl.pallas_call(
        _prep_kernel,
        grid=(b, nc),
        in_specs=[
            pl.BlockSpec((None, CHUNK, NDT), lambda i, c: (i, c, 0)),
            pl.BlockSpec((1, NDT), lambda i, c: (0, 0)),
            pl.BlockSpec((PIECE_W, SSD_GROUPS * CHUNK), lambda i, c: (0, 0)),
        ],
        out_specs=[
            pl.BlockSpec((None, CHUNK, SSD_GROUPS * CHUNK), lambda i, c: (i, c, 0)),
            pl.BlockSpec((None, None, NDT, CHUNK), lambda i, c: (i, c, 0, 0)),
        ],
        out_shape=[
            jax.ShapeDtypeStruct((b, l, SSD_GROUPS * CHUNK), BF16),
            jax.ShapeDtypeStruct((b, nc, NDT, CHUNK), F32),
        ],
        compiler_params=_cparams(2),
        name="ssd_prep",
    )(dt3, a_log, _piece_layout())


def _ssd_kernel(x_ref, b_ref, c_ref, z_ref, pcs_ref, acsr_ref, lay_ref, dskip_ref, ng_ref, o_ref,
                yp_ref, stf_ref, stb_ref, *, nc):
    ii = lax.broadcasted_iota(jnp.int32, (CHUNK, CHUNK), 0)
    jj = lax.broadcasted_iota(jnp.int32, (CHUNK, CHUNK), 1)
    keep_f = jj <= ii
    keep_b = jj >= ii
    low_half = lax.broadcasted_iota(jnp.int32, (1, CHUNK), 1) < SSD_HEAD_DIM
    head_of_lane = lax.broadcasted_iota(jnp.int32, (1, GW), 1) // SSD_HEAD_DIM
    stf_ref[...] = jnp.zeros_like(stf_ref)
    stb_ref[...] = jnp.zeros_like(stb_ref)

    def chunk_pass(c, fwd):
        keep, d = (keep_f, 0) if fwd else (keep_b, 1)
        st_ref = stf_ref if fwd else stb_ref
        rows = pl.ds(pl.multiple_of(c * CHUNK, CHUNK), CHUNK)
        sp = jnp.dot(pcs_ref[rows, :], lay_ref[d], preferred_element_type=F32)
        acs_w = [sp[:, r * CHUNK:(r + 1) * CHUNK] for r in range(SSD_HPG)]
        dt_e = sp[:, SSD_HPG * CHUNK:]
        acs_e = jnp.concatenate([jnp.where(low_half, acs_w[0], acs_w[1]),
                                 jnp.where(low_half, acs_w[2], acs_w[3])], axis=-1)
        acs_r = acsr_ref[c]

        xf = x_ref[rows, :].astype(F32)
        bm = b_ref[rows, :]
        cm = c_ref[rows, :]
        xdt = xf * dt_e
        xdt_b = xdt.astype(BF16)
        scores = lax.dot_general(cm, bm, (((1,), (1,)), ((), ())), preferred_element_type=F32)

        gs, xs = [], []
        for r in range(SSD_HPG):
            seg = acs_w[r] - acs_r[SSD_HPG * d + r:SSD_HPG * d + r + 1, :]
            gs.append((scores * jnp.exp(jnp.where(keep, seg, NEG))).astype(BF16))
            xs.append(jnp.where(head_of_lane == r, xdt_b, jnp.zeros_like(xdt_b)))
        y = jnp.dot(jnp.concatenate(gs, axis=-1), jnp.concatenate(xs, axis=0),
                    preferred_element_type=F32)

        st = st_ref[...]
        y = y + jnp.dot(cm, st.astype(BF16), preferred_element_type=F32) * jnp.exp(acs_e)
        total = acs_e[CHUNK - 1:CHUNK, :] if fwd else acs_e[0:1, :]
        xw = (xdt * jnp.exp(total - acs_e)).astype(BF16)
        st_ref[...] = st * jnp.exp(total) + lax.dot_general(
            bm, xw, (((0,), (0,)), ((), ())), preferred_element_type=F32)
        return rows, y, xf

    def first_visit(c, fwd):
        rows, y, _ = chunk_pass(c, fwd)
        yp_ref[rows, :] = y

    def second_visit(c, fwd):
        rows, y, xf = chunk_pass(c, fwd)
        yt = (yp_ref[rows, :] + y + xf * dskip_ref[...]) * _silu(z_ref[rows, :].astype(F32))
        o_ref[rows, :] = _rms(yt, ng_ref[...]).astype(BF16)

    def body_a(t, carry):
        first_visit(t, True)
        first_visit(nc - 1 - t, False)
        return carry

    def body_b(t, carry):
        second_visit(t, True)
        second_visit(nc - 1 - t, False)
        return carry

    lax.fori_loop(0, nc // 2, body_a, 0)
    lax.fori_loop(nc // 2, nc, body_b, 0)


def _ssd(xbc3, packed3, pcs3, acsr4, d_skip_e, norm_g):
    b, l, _ = xbc3.shape
    nc = l // CHUNK
    assert nc % 2 == 0
    b0 = SSD_INNER // SSD_STATE
    seq = lambda w, j0: pl.BlockSpec((None, l, w), lambda i, g: (i, 0, j0 + g))
    return pl.pallas_call(
        functools.partial(_ssd_kernel, nc=nc),
        grid=(b, SSD_GROUPS),
        in_specs=[
            seq(GW, 0), seq(SSD_STATE, b0), seq(SSD_STATE, b0 + SSD_GROUPS), seq(GW, OFF_Z // GW),
            seq(CHUNK, 0),
            pl.BlockSpec((None, nc, ND, CHUNK), lambda i, g: (i, 0, g, 0)),
            pl.BlockSpec((2, CHUNK, SPREAD_W), lambda i, g: (0, 0, 0)),
            pl.BlockSpec((1, GW), lambda i, g: (0, g)),
            pl.BlockSpec((1, GW), lambda i, g: (0, g)),
        ],
        out_specs=seq(GW, 0),
        out_shape=jax.ShapeDtypeStruct((b, l, SSD_INNER), BF16),
        scratch_shapes=[pltpu.VMEM((l, GW), F32), pltpu.VMEM((SSD_STATE, GW), F32),
                        pltpu.VMEM((SSD_STATE, GW), F32)],
        compiler_params=_cparams(2),
        name="ssd",
    )(xbc3, xbc3, xbc3, packed3, pcs3, acsr4, _spread_layout(), d_skip_e, norm_g)


MERGE_TM = 512


def _merge_kernel(att_ref, y_ref, ga_ref, gs_ref, x_ref, ba_ref, bs_ref, wa_ref, ws_ref, wo_ref, o_ref):
    m1 = jnp.dot(att_ref[...], wa_ref[...], preferred_element_type=F32)
    m2 = jnp.dot(y_ref[...], ws_ref[...], preferred_element_type=F32)
    merged = (jax.nn.sigmoid(ga_ref[...].astype(F32) + ba_ref[...]) * m1
              + jax.nn.sigmoid(gs_ref[...].astype(F32) + bs_ref[...]) * m2)
    o_ref[...] = x_ref[...] + jnp.dot(merged.astype(BF16), wo_ref[...], preferred_element_type=F32)


def _merge(att2, y2, packed2, x2, b_gate, wa, ws, wo):
    m = x2.shape[0]
    tm = min(MERGE_TM, m)
    row = lambda w, j=0: pl.BlockSpec((tm, w), lambda i: (i, j))
    full = lambda r, c, j=0: pl.BlockSpec((r, c), lambda i: (0, j))
    return pl.pallas_call(
        _merge_kernel,
        grid=(m // tm,),
        in_specs=[
            row(ATT_W), row(SSD_INNER), row(D_MODEL, OFF_GA // D_MODEL), row(D_MODEL, OFF_GS // D_MODEL),
            row(D_MODEL), full(1, D_MODEL, 0), full(1, D_MODEL, 1),
            full(ATT_W, D_MODEL), full(SSD_INNER, D_MODEL), full(D_MODEL, D_MODEL),
        ],
        out_specs=row(D_MODEL),
        out_shape=jax.ShapeDtypeStruct((m, D_MODEL), F32),
        compiler_params=_cparams(1),
        name="merge",
    )(att2, y2, packed2, packed2, x2, b_gate, b_gate, wa, ws, wo)


FFN_TM = 512
FFN_TF = 1408
FFN_HALO = 8


def _ffn_kernel(prev_ref, x_ref, next_ref, g_ref, wa_ref, wg_ref, cwa_ref, cwg_ref, cba_ref, cbg_ref,
                wd_ref, o_ref, h_ref, acc_ref):
    m = pl.program_id(1)
    f = pl.program_id(2)
    tm = x_ref.shape[0]

    @pl.when(f == 0)
    def _():
        prev = jnp.where(m > 0, prev_ref[...], 0.0)
        nxt = jnp.where(m < pl.num_programs(1) - 1, next_ref[...], 0.0)
        ext = jnp.concatenate([prev, x_ref[...], nxt], axis=0)
        h_ref[...] = _rms(ext, g_ref[...]).astype(BF16)
        acc_ref[...] = jnp.zeros_like(acc_ref)

    h = h_ref[...]

    def conv(w_ref, cw_ref, cb_ref):
        u = jnp.dot(h, w_ref[...], preferred_element_type=F32)
        out = jnp.broadcast_to(cb_ref[...], (tm, u.shape[1]))
        for k in range(FFN_CONV):
            out = out + _roll_rows(u, k - FFN_CONV // 2, FFN_HALO, tm) * cw_ref[k:k + 1, :]
        return out

    act = conv(wa_ref, cwa_ref, cba_ref) * _silu(conv(wg_ref, cwg_ref, cbg_ref))
    acc_ref[...] += jnp.dot(act.astype(BF16), wd_ref[...], preferred_element_type=F32)

    @pl.when(f == pl.num_programs(2) - 1)
    def _():
        o_ref[...] = x_ref[...] + acc_ref[...]


def _ffn(x3, g_ffn, w_up, conv_w, conv_b, w_down):
    b, l, _ = x3.shape
    tm = min(FFN_TM, l)
    nf = D_FF // FFN_TF
    hb = tm // FFN_HALO
    wcol = lambda r, j0: pl.BlockSpec((r, FFN_TF), lambda i, m, f: (0, j0 + f))
    return pl.pallas_call(
        _ffn_kernel,
        grid=(b, l // tm, nf),
        in_specs=[
            pl.BlockSpec((None, FFN_HALO, D_MODEL), lambda i, m, f: (i, jnp.maximum(m * hb - 1, 0), 0)),
            pl.BlockSpec((None, tm, D_MODEL), lambda i, m, f: (i, m, 0)),
            pl.BlockSpec((None, FFN_HALO, D_MODEL),
                         lambda i, m, f: (i, jnp.minimum((m + 1) * hb, l // FFN_HALO - 1), 0)),
            pl.BlockSpec((1, D_MODEL), lambda i, m, f: (0, 0)),
            wcol(D_MODEL, 0), wcol(D_MODEL, nf),
            wcol(FFN_CONV, 0), wcol(FFN_CONV, nf),
            wcol(1, 0), wcol(1, nf),
            pl.BlockSpec((FFN_TF, D_MODEL), lambda i, m, f: (f, 0)),
        ],
        out_specs=pl.BlockSpec((None, tm, D_MODEL), lambda i, m, f: (i, m, 0)),
        out_shape=jax.ShapeDtypeStruct((b, l, D_MODEL), F32),
        scratch_shapes=[pltpu.VMEM((tm + 2 * FFN_HALO, D_MODEL), BF16), pltpu.VMEM((tm, D_MODEL), F32)],
        compiler_params=_cparams(3),
        name="ffn",
    )(x3, x3, x3, g_ffn, w_up, w_up, conv_w, conv_w, conv_b, conv_b, w_down)


def _layer(x, bias_tab, p):
    b, l, _ = x.shape
    x2 = x.reshape(b * l, D_MODEL)
    packed2, dt2 = _proj(x2, p["g_mix"], p["w_main"], p["w_dt"], p["dt_bias"])
    packed3 = packed2.reshape(b, l, PROJ_W)
    att = _attention(packed3, p["q_norm_g"], p["k_norm_g"], bias_tab)
    xbc = _ssd_conv(packed3, p["ssd_conv_w"], p["ssd_conv_b"])
    pcs, acsr = _ssd_prep(dt2.reshape(b, l, NDT), p["a_log"])
    y = _ssd(xbc, packed3, pcs, acsr, p["d_skip_e"], p["ssd_norm_g"])
    x1 = _merge(att.reshape(b * l, ATT_W), y.reshape(b * l, SSD_INNER), packed2, x2,
                p["b_gate"], p["w_att_out"], p["w_ssd_out"], p["w_o"])
    return _ffn(x1.reshape(b, l, D_MODEL), p["g_ffn"], p["w_up"], p["ffn_conv_w"], p["ffn_conv_b"],
                p["w_down"])


def kernel(x_prompt, x_sample, g_mix, w_in, b_gate, q_norm_g, k_norm_g, rel_bias, ssd_conv_w, ssd_conv_b,
           dt_bias_f, dt_bias_b, A_log_f, A_log_b, D_skip, ssd_norm_g, w_att_out, w_ssd_out, w_o, g_ffn,
           w_up, ffn_conv_w, ffn_conv_b, w_down):
    depth = g_mix.shape[0]
    xs = [x_prompt, x_sample]
    perm = jnp.asarray(_group_major_perm(), jnp.int32)
    for layer in range(depth):
        w = w_in[layer]
        p = {
            "g_mix": g_mix[layer][None, :],
            "w_main": jnp.concatenate([w[:, :DT_COL], w[:, DT_COL + 2 * SSD_HEADS:]], axis=1).astype(BF16),
            "w_dt": w[:, DT_COL:DT_COL + 2 * SSD_HEADS][:, perm].astype(BF16),
            "dt_bias": jnp.concatenate([dt_bias_f[layer], dt_bias_b[layer]])[perm][None, :],
            "q_norm_g": q_norm_g[layer],
            "k_norm_g": k_norm_g[layer],
            "ssd_conv_w": ssd_conv_w[layer],
            "ssd_conv_b": ssd_conv_b[layer][None, :],
            "a_log": jnp.concatenate([A_log_f[layer], A_log_b[layer]])[perm][None, :],
            "d_skip_e": jnp.repeat(D_skip[layer], SSD_HEAD_DIM)[None, :],
            "ssd_norm_g": ssd_norm_g[layer][None, :],
            "b_gate": b_gate[layer][None, :],
            "w_att_out": w_att_out[layer].astype(BF16),
            "w_ssd_out": w_ssd_out[layer].astype(BF16),
            "w_o": w_o[layer].astype(BF16),
            "g_ffn": g_ffn[layer][None, :],
            "w_up": w_up[layer].astype(BF16),
            "ffn_conv_w": ffn_conv_w[layer],
            "ffn_conv_b": ffn_conv_b[layer][None, :],
            "w_down": w_down[layer].astype(BF16),
        }
        bias_tab = _bias_table(rel_bias[layer])
        xs = [_layer(x, bias_tab, p) for x in xs]
    return tuple(xs)
```

```python
import functools

import jax
import jax.numpy as jnp
from jax import lax
from jax.experimental import pallas as pl
from jax.experimental.pallas import tpu as pltpu

F32 = jnp.float32
BF16 = jnp.bfloat16
HIGHEST = lax.Precision.HIGHEST

D_MODEL = 1024
GRID_W = 64
WIN_R = 8
WIN_C = 16
ATT_HEADS = 16
HEAD_DIM = 64
ATT_W = ATT_HEADS * HEAD_DIM
SSD_INNER = 2 * D_MODEL
SSD_HEAD_DIM = 64
SSD_HEADS = SSD_INNER // SSD_HEAD_DIM
SSD_GROUPS = 8
SSD_HPG = SSD_HEADS // SSD_GROUPS
SSD_STATE = 128
SSD_CONV = 5
CONV_DIM = SSD_INNER + 2 * SSD_GROUPS * SSD_STATE
CHUNK = 128
D_FF = 2816
FFN_CONV = 3
EPS = 1e-6

OFF_Q = 0
OFF_V = OFF_Q + ATT_W
OFF_Z = OFF_V + ATT_W
OFF_XBC = OFF_Z + SSD_INNER
OFF_GA = OFF_XBC + CONV_DIM
OFF_GS = OFF_GA + D_MODEL
PROJ_W = OFF_GS + D_MODEL
DT_COL = 3 * ATT_W + SSD_INNER + CONV_DIM

NEG = -1e30
VMEM_LIMIT = 56 * 1024 * 1024


def _cparams(n_axes):
    return pltpu.CompilerParams(
        dimension_semantics=("arbitrary",) * n_axes, vmem_limit_bytes=VMEM_LIMIT)


def _rms(x, g):
    return x * lax.rsqrt(jnp.mean(x * x, axis=-1, keepdims=True) + EPS) * g


def _silu(x):
    return x * jax.nn.sigmoid(x)


PROJ_TM = 1024
PROJ_TN = 1024


PROJ_NT = PROJ_W // PROJ_TN
KT_COLS = 256


def _proj_kernel(x_ref, g_ref, w_ref, wdt_ref, dtb_ref, wkt_ref, gk_ref, o_ref, dt_ref, kt_ref, h_ref):
    j = pl.program_id(1)

    @pl.when(j == 0)
    def _():
        h = _rms(x_ref[...], g_ref[...]).astype(BF16)
        h_ref[...] = h
        raw = jnp.dot(h, wdt_ref[...], preferred_element_type=F32)
        dt_ref[...] = jax.nn.softplus(raw + dtb_ref[...])

    @pl.when(j < PROJ_NT)
    def _():
        o_ref[...] = jnp.dot(h_ref[...], w_ref[...], preferred_element_type=F32).astype(BF16)

    @pl.when(j == PROJ_NT)
    def _():
        gk = gk_ref[...].reshape(1, HEAD_DIM, 1)
        for c in range(h_ref.shape[0] // KT_COLS):
            cols = slice(c * KT_COLS, (c + 1) * KT_COLS)
            kt = lax.dot_general(wkt_ref[...], h_ref[cols, :], (((1,), (1,)), ((), ())),
                                 preferred_element_type=F32)
            kt = kt.reshape(ATT_HEADS, HEAD_DIM, KT_COLS)
            ms = jnp.mean(kt * kt, axis=1, keepdims=True)
            kt_ref[:, cols] = (kt * lax.rsqrt(ms + EPS) * gk).reshape(ATT_W, KT_COLS).astype(BF16)


def _proj(x2, g_mix, w_main, w_dt, dt_bias, w_kt, k_norm_g):
    m = x2.shape[0]
    tm = min(PROJ_TM, m)
    last = PROJ_NT - 1
    return pl.pallas_call(
        _proj_kernel,
        grid=(m // tm, PROJ_NT + 1),
        in_specs=[
            pl.BlockSpec((tm, D_MODEL), lambda i, j: (i, 0)),
            pl.BlockSpec((1, D_MODEL), lambda i, j: (0, 0)),
            pl.BlockSpec((D_MODEL, PROJ_TN), lambda i, j: (0, jnp.minimum(j, last))),
            pl.BlockSpec((D_MODEL, 2 * SSD_HEADS), lambda i, j: (0, 0)),
            pl.BlockSpec((1, 2 * SSD_HEADS), lambda i, j: (0, 0)),
            pl.BlockSpec((ATT_W, D_MODEL), lambda i, j: (0, 0)),
            pl.BlockSpec((HEAD_DIM, 1), lambda i, j: (0, 0)),
        ],
        out_specs=[
            pl.BlockSpec((tm, PROJ_TN), lambda i, j: (i, jnp.minimum(j, last))),
            pl.BlockSpec((tm, 2 * SSD_HEADS), lambda i, j: (i, 0)),
            pl.BlockSpec((ATT_W, tm), lambda i, j: (0, i)),
        ],
        out_shape=[
            jax.ShapeDtypeStruct((m, PROJ_W), BF16),
            jax.ShapeDtypeStruct((m, 2 * SSD_HEADS), F32),
            jax.ShapeDtypeStruct((ATT_W, m), BF16),
        ],
        scratch_shapes=[pltpu.VMEM((tm, D_MODEL), BF16)],
        compiler_params=_cparams(2),
        name="proj",
    )(x2, g_mix, w_main, w_dt, dt_bias, w_kt, k_norm_g.reshape(HEAD_DIM, 1))


N_ROFF = 2 * WIN_R - 1
N_COFF = 2 * WIN_C - 1
KEYS = WIN_R * GRID_W


def _bias_kernel(rb_ref, o_ref):
    h = pl.program_id(0)
    d = pl.program_id(1)
    c = lax.broadcasted_iota(jnp.int32, (GRID_W, GRID_W), 0)
    cp = lax.broadcasted_iota(jnp.int32, (GRID_W, GRID_W), 1)
    cs = jnp.clip(c - WIN_C // 2, 0, GRID_W - WIN_C)
    inside = jnp.where(cp >= cs, jnp.where(cp < cs + WIN_C, 1, 0), 0)
    idx = jnp.clip(cp - c + WIN_C - 1, 0, N_COFF - 1)
    for i in range(WIN_R):
        base = (h * N_ROFF + (i - d + WIN_R - 1)) * N_COFF

        def body(k, acc, base=base):
            return jnp.where(idx == k, rb_ref[base + k], acc)

        t = lax.fori_loop(0, N_COFF, body, jnp.zeros((GRID_W, GRID_W), F32))
        o_ref[0, 0, :, i * GRID_W:(i + 1) * GRID_W] = jnp.where(inside == 1, t, NEG)


def _bias_table(rel_bias):
    return pl.pallas_call(
        _bias_kernel,
        grid=(ATT_HEADS, WIN_R),
        in_specs=[pl.BlockSpec(memory_space=pltpu.SMEM)],
        out_specs=pl.BlockSpec((1, 1, GRID_W, KEYS), lambda h, d: (h // 2, d, h % 2, 0)),
        out_shape=jax.ShapeDtypeStruct((ATT_HEADS // 2, WIN_R, 2 * GRID_W, KEYS), F32),
        compiler_params=_cparams(2),
        name="bias_table",
    )(rel_bias.reshape(-1))


NORM_ROWS = 512


ATT_ROWS = 4
SHIFT_COLS = 256


def _aligned(x, m):
    return x if isinstance(x, int) else pl.multiple_of(x, m)


def _attn_kernel(q_ref, kt_in_ref, v_ref, gq_ref, bias_ref, o_ref, qn_ref, kt_ref, vx_ref, s_ref, p_ref, *, rows):
    l = rows * GRID_W
    pair = 2 * HEAD_DIM
    lane = lax.broadcasted_iota(jnp.int32, (2 * pair, pair), 0) % pair // HEAD_DIM
    lane_t = lax.broadcasted_iota(jnp.int32, (2 * pair, pair), 1) // HEAD_DIM
    seg = jnp.where(lane == lane_t, 1.0 / HEAD_DIM, 0.0).astype(BF16)

    def norm_body(i, carry):
        sl = pl.ds(pl.multiple_of(i * NORM_ROWS, NORM_ROWS), NORM_ROWS)
        x = q_ref[sl, :].astype(F32)
        sq = x * x
        hi = sq.astype(BF16)
        lo = (sq - hi.astype(F32)).astype(BF16)
        ms = jnp.dot(jnp.concatenate([hi, lo], axis=-1), seg, preferred_element_type=F32)
        qn_ref[sl, :] = (x * lax.rsqrt(ms + EPS) * gq_ref[...] * (HEAD_DIM ** -0.5)).astype(BF16)
        vx_ref[sl, 0:pair] = v_ref[sl, :]
        vx_ref[sl, pair:2 * pair] = jnp.ones((NORM_ROWS, pair), BF16)
        return carry

    lax.fori_loop(0, l // NORM_ROWS, norm_body, 0)

    ext = SHIFT_COLS + 2 * GRID_W

    def shifted(c0):
        words = pltpu.bitcast(kt_in_ref[:, pl.ds(c0, ext)], jnp.uint32)
        return pltpu.bitcast(pltpu.roll(words, ext - GRID_W, 1), BF16)

    def shift_body(i, carry):
        c0 = pl.multiple_of(i * SHIFT_COLS, SHIFT_COLS)
        kt_ref[0, :, pl.ds(c0, SHIFT_COLS)] = kt_in_ref[:, pl.ds(c0, SHIFT_COLS)]
        kt_ref[1, :, pl.ds(c0, SHIFT_COLS)] = shifted(c0)[:, :SHIFT_COLS]
        return carry

    n_full = (l - ext) // SHIFT_COLS + 1
    lax.fori_loop(0, n_full, shift_body, 0)
    for c0 in range(n_full * SHIFT_COLS, l, SHIFT_COLS):
        kt_ref[0, :, c0:c0 + SHIFT_COLS] = kt_in_ref[:, c0:c0 + SHIFT_COLS]
    kt_ref[1, :, l - SHIFT_COLS:l - GRID_W] = shifted(l - ext)[:, 2 * GRID_W:ext - GRID_W]

    top = lax.broadcasted_iota(jnp.int32, (1, pair), 1) < HEAD_DIM

    def window(r):
        if isinstance(r, int):
            rs = min(max(r - WIN_R // 2, 0), rows - WIN_R)
            return rs, r - rs, rs % 2
        rs = jnp.clip(r - WIN_R // 2, 0, rows - WIN_R)
        return rs, r - rs, lax.rem(rs, 2)

    def scores_stage(r, slot):
        rs, d, odd = window(r)
        q2 = qn_ref[pl.ds(_aligned(r * GRID_W, GRID_W), GRID_W), :]
        zero = jnp.zeros_like(q2)
        qq = jnp.concatenate([jnp.where(top, q2, zero), jnp.where(top, zero, q2)], axis=0)
        kcol = pl.ds(_aligned((rs - odd) * GRID_W, 2 * GRID_W), KEYS)
        s_ref[slot] = jnp.dot(qq, kt_ref[odd, :, kcol], preferred_element_type=F32) + bias_ref[d]

    def softmax_stage(slot):
        s = s_ref[slot]
        p_ref[slot] = jnp.exp(s - jnp.max(s, axis=-1, keepdims=True)).astype(BF16)

    def values_stage(r, slot):
        rs, _, _ = window(r)
        vrow = pl.ds(_aligned(rs * GRID_W, GRID_W), KEYS)
        ov = jnp.dot(p_ref[slot], vx_ref[vrow, :], preferred_element_type=F32)
        num = jnp.where(top, ov[:GRID_W, :pair], ov[GRID_W:, :pair])
        den = jnp.where(top, ov[:GRID_W, pair:], ov[GRID_W:, pair:])
        o_ref[pl.ds(_aligned(r * GRID_W, GRID_W), GRID_W), :] = (num / den).astype(BF16)

    def stages(t):
        n_groups = rows // ATT_ROWS
        static = isinstance(t, int)
        for u in range(ATT_ROWS):
            if not static or 0 <= t - 2 < n_groups:
                values_stage((t - 2) * ATT_ROWS + u, u)
        for u in range(ATT_ROWS):
            if not static or 0 <= t - 1 < n_groups:
                softmax_stage(u)
        for u in range(ATT_ROWS):
            if not static or 0 <= t < n_groups:
                scores_stage(t * ATT_ROWS + u, u)

    n_groups = rows // ATT_ROWS
    stages(0)
    stages(1)

    def body(t, carry):
        stages(t)
        return carry

    lax.fori_loop(2, n_groups, body, 0)
    stages(n_groups)
    stages(n_groups + 1)


def _attention(packed3, kt2, q_norm_g, bias_tab):
    b, l, _ = packed3.shape
    rows = l // GRID_W
    assert rows % ATT_ROWS == 0 and rows >= 2 * ATT_ROWS and l % SHIFT_COLS == 0
    pair = 2 * HEAD_DIM
    gq = jnp.tile(q_norm_g.reshape(1, HEAD_DIM), (1, 2))
    col = lambda off: (lambda i, j: (i, 0, off // pair + j))
    return pl.pallas_call(
        functools.partial(_attn_kernel, rows=rows),
        grid=(b, ATT_HEADS // 2),
        in_specs=[
            pl.BlockSpec((None, l, pair), col(OFF_Q)),
            pl.BlockSpec((pair, l), lambda i, j: (j, i)),
            pl.BlockSpec((None, l, pair), col(OFF_V)),
            pl.BlockSpec((1, pair), lambda i, j: (0, 0)),
            pl.BlockSpec((None, WIN_R, 2 * GRID_W, KEYS), lambda i, j: (j, 0, 0, 0)),
        ],
        out_specs=pl.BlockSpec((None, l, pair), lambda i, j: (i, 0, j)),
        out_shape=jax.ShapeDtypeStruct((b, l, ATT_W), BF16),
        scratch_shapes=[pltpu.VMEM((l, pair), BF16), pltpu.VMEM((2, pair, l), BF16),
                        pltpu.VMEM((l, 2 * pair), BF16), pltpu.VMEM((ATT_ROWS, 2 * GRID_W, KEYS), F32),
                        pltpu.VMEM((ATT_ROWS, 2 * GRID_W, KEYS), BF16)],
        compiler_params=_cparams(2),
        name="attention",
    )(packed3, kt2, packed3, gq, bias_tab)


CONV_TN = 512
CONV_ROWS = 128
HALO = 16


def _roll_rows(x, offset, lo, n):
    if offset == 0:
        return x[lo:lo + n]
    return pltpu.roll(x, (-offset) % x.shape[0], 0)[lo:lo + n]


def _conv_kernel(x_ref, w_ref, b_ref, o_ref):
    l = x_ref.shape[0]
    n_steps = l // CONV_ROWS
    half = SSD_CONV // 2
    taps = [k for k in range(SSD_CONV) if k != half]
    ext_rows = CONV_ROWS + 2 * HALO
    ri = lax.broadcasted_iota(jnp.int32, (len(taps) * CONV_ROWS, ext_rows), 0)
    ci = lax.broadcasted_iota(jnp.int32, (len(taps) * CONV_ROWS, ext_rows), 1)
    src = ri % CONV_ROWS + HALO - half + ri // CONV_ROWS
    src = jnp.where(ri // CONV_ROWS >= half, src + 1, src)
    shift = jnp.where(ci == src, 1.0, 0.0).astype(BF16)

    def one(c):
        lo = pl.multiple_of(c * CONV_ROWS, CONV_ROWS)
        lo_before = pl.multiple_of(jnp.maximum(lo - HALO, 0), HALO)
        lo_after = pl.multiple_of(jnp.minimum(lo + CONV_ROWS, l - HALO), HALO)
        main = x_ref[pl.ds(lo, CONV_ROWS), :]
        before = x_ref[pl.ds(lo_before, HALO), :]
        after = x_ref[pl.ds(lo_after, HALO), :]
        zero = jnp.zeros_like(before)
        ext = jnp.concatenate([jnp.where(c > 0, before, zero), main,
                               jnp.where(c < n_steps - 1, after, zero)], axis=0)
        sh = jnp.dot(shift, ext, preferred_element_type=F32)
        acc = b_ref[...] + main.astype(F32) * w_ref[half:half + 1, :]
        for t, k in enumerate(taps):
            acc = acc + sh[t * CONV_ROWS:(t + 1) * CONV_ROWS] * w_ref[k:k + 1, :]
        o_ref[pl.ds(lo, CONV_ROWS), :] = _silu(acc).astype(BF16)

    def body(t, carry):
        one(2 * t)
        one(2 * t + 1)
        return carry

    lax.fori_loop(0, n_steps // 2, body, 0)


def _ssd_conv(packed3, conv_w, conv_b):
    b, l, _ = packed3.shape
    assert l % (2 * CONV_ROWS) == 0
    c0 = OFF_XBC // CONV_TN
    return pl.pallas_call(
        _conv_kernel,
        grid=(b, CONV_DIM // CONV_TN),
        in_specs=[
            pl.BlockSpec((None, l, CONV_TN), lambda i, j: (i, 0, c0 + j)),
            pl.BlockSpec((SSD_CONV, CONV_TN), lambda i, j: (0, j)),
            pl.BlockSpec((1, CONV_TN), lambda i, j: (0, j)),
        ],
        out_specs=pl.BlockSpec((None, l, CONV_TN), lambda i, j: (i, 0, j)),
        out_shape=jax.ShapeDtypeStruct((b, l, CONV_DIM), BF16),
        compiler_params=_cparams(2),
        name="ssd_conv",
    )(packed3, conv_w, conv_b)


GW = SSD_HPG * SSD_HEAD_DIM


def _split3(v):
    hi = v.astype(BF16)
    r1 = v - hi.astype(F32)
    mid = r1.astype(BF16)
    lo = (r1 - mid.astype(F32)).astype(BF16)
    return hi, mid, lo


ND = 2 * SSD_HPG
NDT = 2 * SSD_HEADS
PIECE_W = 6 * NDT
SPREAD_W = SSD_HPG * CHUNK + GW


def _group_major_perm():
    return [d * SSD_HEADS + g * SSD_HPG + r for g in range(SSD_GROUPS) for d in range(2) for r in range(SSD_HPG)]


def _piece_layout():
    import numpy as np
    m = np.zeros((PIECE_W, SSD_GROUPS * CHUNK), np.float32)
    for q in range(6):
        for g in range(SSD_GROUPS):
            for k in range(ND):
                m[q * NDT + ND * g + k, CHUNK * g + ND * q + k] = 1.0
    return jnp.asarray(m, BF16)


def _spread_layout():
    import numpy as np
    m = np.zeros((2, CHUNK, SPREAD_W), np.float32)
    for d in range(2):
        for q in range(6):
            for r in range(SSD_HPG):
                row = ND * q + SSD_HPG * d + r
                if q < 3:
                    m[d, row, CHUNK * r:CHUNK * (r + 1)] = 1.0
                else:
                    m[d, row, SSD_HPG * CHUNK + SSD_HEAD_DIM * r:SSD_HPG * CHUNK + SSD_HEAD_DIM * (r + 1)] = 1.0
    return jnp.asarray(m, BF16)


def _prep_kernel(dt_ref, alog_ref, lay_ref, pcs_ref, acsr_ref):
    neg_a = -jnp.exp(alog_ref[...])
    ii = lax.broadcasted_iota(jnp.int32, (CHUNK, CHUNK), 0)
    jj = lax.broadcasted_iota(jnp.int32, (CHUNK, CHUNK), 1)
    tri_f = jnp.where(jj <= ii, 1.0, 0.0).astype(BF16)
    tri_b = jnp.where(jj >= ii, 1.0, 0.0).astype(BF16)
    col = lax.broadcasted_iota(jnp.int32, (1, 3 * NDT), 1)
    is_fwd = (col % ND) < SSD_HPG
    er = lax.broadcasted_iota(jnp.int32, (NDT, NDT), 0)
    ec = lax.broadcasted_iota(jnp.int32, (NDT, NDT), 1)
    eye = jnp.where(er == ec, 1.0, 0.0).astype(BF16)
    for c in range(PREP_CHUNKS):
        rows = slice(c * CHUNK, (c + 1) * CHUNK)
        dt = dt_ref[rows, :]
        a3 = jnp.concatenate(_split3(dt * neg_a), axis=-1)
        s3 = jnp.where(is_fwd, jnp.dot(tri_f, a3, preferred_element_type=F32),
                       jnp.dot(tri_b, a3, preferred_element_type=F32))
        acs = s3[:, 0:NDT] + s3[:, NDT:2 * NDT] + s3[:, 2 * NDT:3 * NDT]
        acs3 = _split3(acs)
        pieces = jnp.concatenate(acs3 + _split3(dt), axis=-1)
        pcs_ref[rows, :] = jnp.dot(pieces, lay_ref[...], preferred_element_type=F32).astype(BF16)
        tr = None
        for p in acs3:
            d = lax.dot_general(eye, p, (((1,), (1,)), ((), ())), preferred_element_type=F32)
            tr = d if tr is None else tr + d
        acsr_ref[c] = tr


PREP_CHUNKS = 4


def _ssd_prep(dt3, a_log):
    b, l, _ = dt3.shape
    nc = l // CHUNK
    assert nc % PREP_CHUNKS == 0
    rows = PREP_CHUNKS * CHUNK
    return pl.pallas_call(
        _prep_kernel,
        grid=(b, nc // PREP_CHUNKS),
        in_specs=[
            pl.BlockSpec((None, rows, NDT), lambda i, c: (i, c, 0)),
            pl.BlockSpec((1, NDT), lambda i, c: (0, 0)),
            pl.BlockSpec((PIECE_W, SSD_GROUPS * CHUNK), lambda i, c: (0, 0)),
        ],
        out_specs=[
            pl.BlockSpec((None, rows, SSD_GROUPS * CHUNK), lambda i, c: (i, c, 0)),
            pl.BlockSpec((None, PREP_CHUNKS, NDT, CHUNK), lambda i, c: (i, c, 0, 0)),
        ],
        out_shape=[
            jax.ShapeDtypeStruct((b, l, SSD_GROUPS * CHUNK), BF16),
            jax.ShapeDtypeStruct((b, nc, NDT, CHUNK), F32),
        ],
        compiler_params=_cparams(2),
        name="ssd_prep",
    )(dt3, a_log, _piece_layout())


def _ssd_kernel(x_ref, b_ref, c_ref, z_ref, pcs_ref, acsr_ref, lay_ref, dskip_ref, ng_ref, o_ref,
                yp_ref, st_ref, g_ref, xdt_ref, xw_ref, ea_ref, et_ref, *, nc):
    ii = lax.broadcasted_iota(jnp.int32, (CHUNK, CHUNK), 0)
    jj = lax.broadcasted_iota(jnp.int32, (CHUNK, CHUNK), 1)
    keep_f = jj <= ii
    keep_b = jj >= ii
    low_half = lax.broadcasted_iota(jnp.int32, (1, CHUNK), 1) < SSD_HEAD_DIM
    head_of_lane = lax.broadcasted_iota(jnp.int32, (1, GW), 1) // SSD_HEAD_DIM
    st_ref[...] = jnp.zeros_like(st_ref)

    def chunk_rows(c):
        return pl.ds(_aligned(c * CHUNK, CHUNK), CHUNK)

    def decay_stage(c, d):
        keep = keep_b if d else keep_f
        rows = chunk_rows(c)
        sp = jnp.dot(pcs_ref[rows, :], lay_ref[d], preferred_element_type=F32)
        acs_w = [sp[:, r * CHUNK:(r + 1) * CHUNK] for r in range(SSD_HPG)]
        dt_e = sp[:, SSD_HPG * CHUNK:]
        acs_e = jnp.concatenate([jnp.where(low_half, acs_w[0], acs_w[1]),
                                 jnp.where(low_half, acs_w[2], acs_w[3])], axis=-1)
        acs_r = acsr_ref[c]
        xdt = x_ref[rows, :].astype(F32) * dt_e
        scores = lax.dot_general(c_ref[rows, :], b_ref[rows, :], (((1,), (1,)), ((), ())),
                                 preferred_element_type=F32)
        for r in range(SSD_HPG):
            seg = acs_w[r] - acs_r[SSD_HPG * d + r:SSD_HPG * d + r + 1, :]
            g_ref[d, :, r * CHUNK:(r + 1) * CHUNK] = (scores * jnp.exp(jnp.where(keep, seg, NEG))).astype(BF16)
        total = acs_e[0:1, :] if d else acs_e[CHUNK - 1:CHUNK, :]
        xdt_ref[d] = xdt.astype(BF16)
        xw_ref[d] = (xdt * jnp.exp(total - acs_e)).astype(BF16)
        ea_ref[d] = jnp.exp(acs_e)
        et_ref[d] = jnp.exp(total)

    def state_stage(c, d):
        rows = chunk_rows(c)
        xdt_b = xdt_ref[d]
        zero = jnp.zeros_like(xdt_b)
        xbd = jnp.concatenate([jnp.where(head_of_lane == r, xdt_b, zero) for r in range(SSD_HPG)], axis=0)
        cm = c_ref[rows, :]
        st = st_ref[d]
        y = (jnp.dot(g_ref[d], xbd, preferred_element_type=F32)
             + jnp.dot(cm, st.astype(BF16), preferred_element_type=F32) * ea_ref[d])
        st_ref[d] = st * et_ref[d] + lax.dot_general(
            b_ref[rows, :], xw_ref[d], (((0,), (0,)), ((), ())), preferred_element_type=F32)
        return rows, y

    def first_visit(c, d):
        rows, y = state_stage(c, d)
        yp_ref[rows, :] = y

    def second_visit(c, d):
        rows, y = state_stage(c, d)
        xf = x_ref[rows, :].astype(F32)
        yt = (yp_ref[rows, :] + y + xf * dskip_ref[...]) * _silu(z_ref[rows, :].astype(F32))
        o_ref[rows, :] = _rms(yt, ng_ref[...]).astype(BF16)

    decay_stage(0, 0)
    decay_stage(nc - 1, 1)

    def body_a(t, carry):
        first_visit(t, 0)
        first_visit(nc - 1 - t, 1)
        decay_stage(t + 1, 0)
        decay_stage(nc - 2 - t, 1)
        return carry

    def body_b(t, carry):
        second_visit(t, 0)
        second_visit(nc - 1 - t, 1)
        decay_stage(t + 1, 0)
        decay_stage(nc - 2 - t, 1)
        return carry

    lax.fori_loop(0, nc // 2, body_a, 0)
    lax.fori_loop(nc // 2, nc - 1, body_b, 0)
    second_visit(nc - 1, 0)
    second_visit(0, 1)


def _ssd(xbc3, packed3, pcs3, acsr4, d_skip_e, norm_g):
    b, l, _ = xbc3.shape
    nc = l // CHUNK
    assert nc % 2 == 0
    b0 = SSD_INNER // SSD_STATE
    seq = lambda w, j0: pl.BlockSpec((None, l, w), lambda i, g: (i, 0, j0 + g))
    return pl.pallas_call(
        functools.partial(_ssd_kernel, nc=nc),
        grid=(b, SSD_GROUPS),
        in_specs=[
            seq(GW, 0), seq(SSD_STATE, b0), seq(SSD_STATE, b0 + SSD_GROUPS), seq(GW, OFF_Z // GW),
            seq(CHUNK, 0),
            pl.BlockSpec((None, nc, ND, CHUNK), lambda i, g: (i, 0, g, 0)),
            pl.BlockSpec((2, CHUNK, SPREAD_W), lambda i, g: (0, 0, 0)),
            pl.BlockSpec((1, GW), lambda i, g: (0, g)),
            pl.BlockSpec((1, GW), lambda i, g: (0, g)),
        ],
        out_specs=seq(GW, 0),
        out_shape=jax.ShapeDtypeStruct((b, l, SSD_INNER), BF16),
        scratch_shapes=[
            pltpu.VMEM((l, GW), F32),
            pltpu.VMEM((2, SSD_STATE, GW), F32),
            pltpu.VMEM((2, CHUNK, SSD_HPG * CHUNK), BF16),
            pltpu.VMEM((2, CHUNK, GW), BF16),
            pltpu.VMEM((2, CHUNK, GW), BF16),
            pltpu.VMEM((2, CHUNK, GW), F32),
            pltpu.VMEM((2, 1, GW), F32),
        ],
        compiler_params=_cparams(2),
        name="ssd",
    )(xbc3, xbc3, xbc3, packed3, pcs3, acsr4, _spread_layout(), d_skip_e, norm_g)


MERGE_TM = 512


def _merge_kernel(att_ref, y_ref, ga_ref, gs_ref, x_ref, ba_ref, bs_ref, wa_ref, ws_ref, wo_ref, o_ref):
    m1 = jnp.dot(att_ref[...], wa_ref[...], preferred_element_type=F32)
    m2 = jnp.dot(y_ref[...], ws_ref[...], preferred_element_type=F32)
    merged = (jax.nn.sigmoid(ga_ref[...].astype(F32) + ba_ref[...]) * m1
              + jax.nn.sigmoid(gs_ref[...].astype(F32) + bs_ref[...]) * m2)
    o_ref[...] = x_ref[...] + jnp.dot(merged.astype(BF16), wo_ref[...], preferred_element_type=F32)


def _merge(att2, y2, packed2, x2, b_gate, wa, ws, wo):
    m = x2.shape[0]
    tm = min(MERGE_TM, m)
    row = lambda w, j=0: pl.BlockSpec((tm, w), lambda i: (i, j))
    full = lambda r, c, j=0: pl.BlockSpec((r, c), lambda i: (0, j))
    return pl.pallas_call(
        _merge_kernel,
        grid=(m // tm,),
        in_specs=[
            row(ATT_W), row(SSD_INNER), row(D_MODEL, OFF_GA // D_MODEL), row(D_MODEL, OFF_GS // D_MODEL),
            row(D_MODEL), full(1, D_MODEL, 0), full(1, D_MODEL, 1),
            full(ATT_W, D_MODEL), full(SSD_INNER, D_MODEL), full(D_MODEL, D_MODEL),
        ],
        out_specs=row(D_MODEL),
        out_shape=jax.ShapeDtypeStruct((m, D_MODEL), F32),
        compiler_params=_cparams(1),
        name="merge",
    )(att2, y2, packed2, packed2, x2, b_gate, b_gate, wa, ws, wo)


FFN_TM = 512
FFN_TF = 1408
FFN_HALO = 8


def _ffn_kernel(prev_ref, x_ref, next_ref, g_ref, wa_ref, wg_ref, cwa_ref, cwg_ref, cba_ref, cbg_ref,
                wd_ref, o_ref, h_ref, acc_ref):
    m = pl.program_id(1)
    f = pl.program_id(2)
    tm = x_ref.shape[0]

    @pl.when(f == 0)
    def _():
        prev = jnp.where(m > 0, prev_ref[...], 0.0)
        nxt = jnp.where(m < pl.num_programs(1) - 1, next_ref[...], 0.0)
        ext = jnp.concatenate([prev, x_ref[...], nxt], axis=0)
        h_ref[...] = _rms(ext, g_ref[...]).astype(BF16)
        acc_ref[...] = jnp.zeros_like(acc_ref)

    h = h_ref[...]

    def conv(w_ref, cw_ref, cb_ref):
        u = jnp.dot(h, w_ref[...], preferred_element_type=F32)
        out = jnp.broadcast_to(cb_ref[...], (tm, u.shape[1]))
        for k in range(FFN_CONV):
            out = out + _roll_rows(u, k - FFN_CONV // 2, FFN_HALO, tm) * cw_ref[k:k + 1, :]
        return out

    act = conv(wa_ref, cwa_ref, cba_ref) * _silu(conv(wg_ref, cwg_ref, cbg_ref))
    acc_ref[...] += jnp.dot(act.astype(BF16), wd_ref[...], preferred_element_type=F32)

    @pl.when(f == pl.num_programs(2) - 1)
    def _():
        o_ref[...] = x_ref[...] + acc_ref[...]


def _ffn(x3, g_ffn, w_up, conv_w, conv_b, w_down):
    b, l, _ = x3.shape
    tm = min(FFN_TM, l)
    nf = D_FF // FFN_TF
    hb = tm // FFN_HALO
    wcol = lambda r, j0: pl.BlockSpec((r, FFN_TF), lambda i, m, f: (0, j0 + f))
    return pl.pallas_call(
        _ffn_kernel,
        grid=(b, l // tm, nf),
        in_specs=[
            pl.BlockSpec((None, FFN_HALO, D_MODEL), lambda i, m, f: (i, jnp.maximum(m * hb - 1, 0), 0)),
            pl.BlockSpec((None, tm, D_MODEL), lambda i, m, f: (i, m, 0)),
            pl.BlockSpec((None, FFN_HALO, D_MODEL),
                         lambda i, m, f: (i, jnp.minimum((m + 1) * hb, l // FFN_HALO - 1), 0)),
            pl.BlockSpec((1, D_MODEL), lambda i, m, f: (0, 0)),
            wcol(D_MODEL, 0), wcol(D_MODEL, nf),
            wcol(FFN_CONV, 0), wcol(FFN_CONV, nf),
            wcol(1, 0), wcol(1, nf),
            pl.BlockSpec((FFN_TF, D_MODEL), lambda i, m, f: (f, 0)),
        ],
        out_specs=pl.BlockSpec((None, tm, D_MODEL), lambda i, m, f: (i, m, 0)),
        out_shape=jax.ShapeDtypeStruct((b, l, D_MODEL), F32),
        scratch_shapes=[pltpu.VMEM((tm + 2 * FFN_HALO, D_MODEL), BF16), pltpu.VMEM((tm, D_MODEL), F32)],
        compiler_params=_cparams(3),
        name="ffn",
    )(x3, x3, x3, g_ffn, w_up, w_up, conv_w, conv_w, conv_b, conv_b, w_down)


def _layer(x, bias_tab, p):
    b, l, _ = x.shape
    x2 = x.reshape(b * l, D_MODEL)
    packed2, dt2, kt2 = _proj(x2, p["g_mix"], p["w_main"], p["w_dt"], p["dt_bias"], p["w_kt"], p["k_norm_g"])
    packed3 = packed2.reshape(b, l, PROJ_W)
    att = _attention(packed3, kt2, p["q_norm_g"], bias_tab)
    xbc = _ssd_conv(packed3, p["ssd_conv_w"], p["ssd_conv_b"])
    pcs, acsr = _ssd_prep(dt2.reshape(b, l, NDT), p["a_log"])
    y = _ssd(xbc, packed3, pcs, acsr, p["d_skip_e"], p["ssd_norm_g"])
    x1 = _merge(att.reshape(b * l, ATT_W), y.reshape(b * l, SSD_INNER), packed2, x2,
                p["b_gate"], p["w_att_out"], p["w_ssd_out"], p["w_o"])
    return _ffn(x1.reshape(b, l, D_MODEL), p["g_ffn"], p["w_up"], p["ffn_conv_w"], p["ffn_conv_b"],
                p["w_down"])


def kernel(x_prompt, x_sample, g_mix, w_in, b_gate, q_norm_g, k_norm_g, rel_bias, ssd_conv_w, ssd_conv_b,
           dt_bias_f, dt_bias_b, A_log_f, A_log_b, D_skip, ssd_norm_g, w_att_out, w_ssd_out, w_o, g_ffn,
           w_up, ffn_conv_w, ffn_conv_b, w_down):
    depth = g_mix.shape[0]
    xs = [x_prompt, x_sample]
    perm = jnp.asarray(_group_major_perm(), jnp.int32)
    for layer in range(depth):
        w = w_in[layer]
        p = {
            "g_mix": g_mix[layer][None, :],
            "w_main": jnp.concatenate([w[:, :ATT_W], w[:, 2 * ATT_W:DT_COL], w[:, DT_COL + 2 * SSD_HEADS:]],
                                      axis=1).astype(BF16),
            "w_kt": w[:, ATT_W:2 * ATT_W].T.astype(BF16),
            "w_dt": w[:, DT_COL:DT_COL + 2 * SSD_HEADS][:, perm].astype(BF16),
            "dt_bias": jnp.concatenate([dt_bias_f[layer], dt_bias_b[layer]])[perm][None, :],
            "q_norm_g": q_norm_g[layer],
            "k_norm_g": k_norm_g[layer],
            "ssd_conv_w": ssd_conv_w[layer],
            "ssd_conv_b": ssd_conv_b[layer][None, :],
            "a_log": jnp.concatenate([A_log_f[layer], A_log_b[layer]])[perm][None, :],
            "d_skip_e": jnp.repeat(D_skip[layer], SSD_HEAD_DIM)[None, :],
            "ssd_norm_g": ssd_norm_g[layer][None, :],
            "b_gate": b_gate[layer][None, :],
            "w_att_out": w_att_out[layer].astype(BF16),
            "w_ssd_out": w_ssd_out[layer].astype(BF16),
            "w_o": w_o[layer].astype(BF16),
            "g_ffn": g_ffn[layer][None, :],
            "w_up": w_up[layer].astype(BF16),
            "ffn_conv_w": ffn_conv_w[layer],
            "ffn_conv_b": ffn_conv_b[layer][None, :],
            "w_down": w_down[layer].astype(BF16),
        }
        bias_tab = _bias_table(rel_bias[layer])
        xs = [_layer(x, bias_tab, p) for x in xs]
    return tuple(xs)
```

```python
import functools

import jax
import jax.numpy as jnp
from jax import lax
from jax.experimental import pallas as pl
from jax.experimental.pallas import tpu as pltpu

F32 = jnp.float32
BF16 = jnp.bfloat16
HIGHEST = lax.Precision.HIGHEST

D_MODEL = 1024
GRID_W = 64
WIN_R = 8
WIN_C = 16
ATT_HEADS = 16
HEAD_DIM = 64
ATT_W = ATT_HEADS * HEAD_DIM
SSD_INNER = 2 * D_MODEL
SSD_HEAD_DIM = 64
SSD_HEADS = SSD_INNER // SSD_HEAD_DIM
SSD_GROUPS = 8
SSD_HPG = SSD_HEADS // SSD_GROUPS
SSD_STATE = 128
SSD_CONV = 5
CONV_DIM = SSD_INNER + 2 * SSD_GROUPS * SSD_STATE
CHUNK = 128
D_FF = 2816
FFN_CONV = 3
EPS = 1e-6

OFF_Q = 0
OFF_V = OFF_Q + ATT_W
OFF_Z = OFF_V + ATT_W
OFF_XBC = OFF_Z + SSD_INNER
OFF_GA = OFF_XBC + CONV_DIM
OFF_GS = OFF_GA + D_MODEL
PROJ_W = OFF_GS + D_MODEL
DT_COL = 3 * ATT_W + SSD_INNER + CONV_DIM

NEG = -1e30
LOG2E = 1.4426950408889634
VMEM_LIMIT = 56 * 1024 * 1024


def _cparams(n_axes):
    return pltpu.CompilerParams(
        dimension_semantics=("arbitrary",) * n_axes, vmem_limit_bytes=VMEM_LIMIT)


def _rms(x, g):
    return x * lax.rsqrt(jnp.mean(x * x, axis=-1, keepdims=True) + EPS) * g


def _silu(x):
    return x * jax.nn.sigmoid(x)


PROJ_TM = 1024
PROJ_TN = 2048


PROJ_NT = PROJ_W // PROJ_TN
KT_COLS = 256


def _proj_kernel(x_ref, g_ref, w_ref, wdt_ref, dtb_ref, wkt_ref, gk_ref, o_ref, dt_ref, kt_ref, h_ref):
    j = pl.program_id(1)

    @pl.when(j == 0)
    def _():
        h = _rms(x_ref[...], g_ref[...]).astype(BF16)
        h_ref[...] = h
        raw = jnp.dot(h, wdt_ref[...], preferred_element_type=F32)
        dt_ref[...] = jax.nn.softplus(raw + dtb_ref[...])

    @pl.when(j < PROJ_NT)
    def _():
        o_ref[...] = jnp.dot(h_ref[...], w_ref[...], preferred_element_type=F32).astype(BF16)

    @pl.when(j == PROJ_NT)
    def _():
        gk = gk_ref[...].reshape(1, HEAD_DIM, 1)
        for c in range(h_ref.shape[0] // KT_COLS):
            cols = slice(c * KT_COLS, (c + 1) * KT_COLS)
            kt = lax.dot_general(wkt_ref[...], h_ref[cols, :], (((1,), (1,)), ((), ())),
                                 preferred_element_type=F32)
            kt = kt.reshape(ATT_HEADS, HEAD_DIM, KT_COLS)
            ms = jnp.mean(kt * kt, axis=1, keepdims=True)
            kt_ref[:, cols] = (kt * lax.rsqrt(ms + EPS) * gk).reshape(ATT_W, KT_COLS).astype(BF16)


def _proj(x2, g_mix, w_main, w_dt, dt_bias, w_kt, k_norm_g):
    m = x2.shape[0]
    tm = min(PROJ_TM, m)
    last = PROJ_NT - 1
    return pl.pallas_call(
        _proj_kernel,
        grid=(m // tm, PROJ_NT + 1),
        in_specs=[
            pl.BlockSpec((tm, D_MODEL), lambda i, j: (i, 0)),
            pl.BlockSpec((1, D_MODEL), lambda i, j: (0, 0)),
            pl.BlockSpec((D_MODEL, PROJ_TN), lambda i, j: (0, jnp.minimum(j, last))),
            pl.BlockSpec((D_MODEL, 2 * SSD_HEADS), lambda i, j: (0, 0)),
            pl.BlockSpec((1, 2 * SSD_HEADS), lambda i, j: (0, 0)),
            pl.BlockSpec((ATT_W, D_MODEL), lambda i, j: (0, 0)),
            pl.BlockSpec((HEAD_DIM, 1), lambda i, j: (0, 0)),
        ],
        out_specs=[
            pl.BlockSpec((tm, PROJ_TN), lambda i, j: (i, jnp.minimum(j, last))),
            pl.BlockSpec((tm, 2 * SSD_HEADS), lambda i, j: (i, 0)),
            pl.BlockSpec((ATT_W, tm), lambda i, j: (0, i)),
        ],
        out_shape=[
            jax.ShapeDtypeStruct((m, PROJ_W), BF16),
            jax.ShapeDtypeStruct((m, 2 * SSD_HEADS), F32),
            jax.ShapeDtypeStruct((ATT_W, m), BF16),
        ],
        scratch_shapes=[pltpu.VMEM((tm, D_MODEL), BF16)],
        compiler_params=_cparams(2),
        name="proj",
    )(x2, g_mix, w_main, w_dt, dt_bias, w_kt, k_norm_g.reshape(HEAD_DIM, 1))


N_ROFF = 2 * WIN_R - 1
N_COFF = 2 * WIN_C - 1
KEYS = WIN_R * GRID_W


def _bias_kernel(rb_ref, o_ref):
    h = pl.program_id(0)
    c = lax.broadcasted_iota(jnp.int32, (GRID_W, GRID_W), 0)
    cp = lax.broadcasted_iota(jnp.int32, (GRID_W, GRID_W), 1)
    cs = jnp.clip(c - WIN_C // 2, 0, GRID_W - WIN_C)
    inside = jnp.where(cp >= cs, jnp.where(cp < cs + WIN_C, 1, 0), 0)
    idx = jnp.clip(cp - c + WIN_C - 1, 0, N_COFF - 1)
    tiles = []
    for ro in range(N_ROFF):
        base = (h * N_ROFF + ro) * N_COFF

        def body(k, acc, base=base):
            return jnp.where(idx == k, rb_ref[base + k], acc)

        t = lax.fori_loop(0, N_COFF, body, jnp.zeros((GRID_W, GRID_W), F32))
        tiles.append(jnp.where(inside == 1, t, NEG))
    for d in range(WIN_R):
        for i in range(WIN_R):
            o_ref[0, d, :, i * GRID_W:(i + 1) * GRID_W] = tiles[i - d + WIN_R - 1]


def _bias_table(rel_bias):
    return pl.pallas_call(
        _bias_kernel,
        grid=(ATT_HEADS,),
        in_specs=[pl.BlockSpec(memory_space=pltpu.SMEM)],
        out_specs=pl.BlockSpec((1, WIN_R, GRID_W, KEYS), lambda h: (h // 2, 0, h % 2, 0)),
        out_shape=jax.ShapeDtypeStruct((ATT_HEADS // 2, WIN_R, 2 * GRID_W, KEYS), F32),
        compiler_params=_cparams(1),
        name="bias_table",
    )(rel_bias.reshape(-1))


NORM_ROWS = 512


ATT_ROWS = 4
SHIFT_COLS = 256


def _aligned(x, m):
    return x if isinstance(x, int) else pl.multiple_of(x, m)


def _attn_kernel(q_ref, kt_in_ref, v_ref, gq_ref, bias_ref, o_ref, qn_ref, kt_ref, vx_ref, s_ref, p_ref, *, rows):
    l = rows * GRID_W
    pair = 2 * HEAD_DIM
    lane = lax.broadcasted_iota(jnp.int32, (2 * pair, pair), 0) % pair // HEAD_DIM
    lane_t = lax.broadcasted_iota(jnp.int32, (2 * pair, pair), 1) // HEAD_DIM
    seg = jnp.where(lane == lane_t, 1.0 / HEAD_DIM, 0.0).astype(BF16)

    def norm_body(i, carry):
        sl = pl.ds(pl.multiple_of(i * NORM_ROWS, NORM_ROWS), NORM_ROWS)
        x = q_ref[sl, :].astype(F32)
        sq = x * x
        hi = sq.astype(BF16)
        lo = (sq - hi.astype(F32)).astype(BF16)
        ms = jnp.dot(jnp.concatenate([hi, lo], axis=-1), seg, preferred_element_type=F32)
        qn_ref[sl, :] = (x * lax.rsqrt(ms + EPS) * gq_ref[...] * (HEAD_DIM ** -0.5)).astype(BF16)
        vx_ref[sl, 0:pair] = v_ref[sl, :]
        vx_ref[sl, pair:2 * pair] = jnp.ones((NORM_ROWS, pair), BF16)
        return carry

    lax.fori_loop(0, l // NORM_ROWS, norm_body, 0)

    ext = SHIFT_COLS + 2 * GRID_W

    def shifted(c0):
        words = pltpu.bitcast(kt_in_ref[:, pl.ds(c0, ext)], jnp.uint32)
        return pltpu.bitcast(pltpu.roll(words, ext - GRID_W, 1), BF16)

    def shift_body(i, carry):
        c0 = pl.multiple_of(i * SHIFT_COLS, SHIFT_COLS)
        kt_ref[0, :, pl.ds(c0, SHIFT_COLS)] = kt_in_ref[:, pl.ds(c0, SHIFT_COLS)]
        kt_ref[1, :, pl.ds(c0, SHIFT_COLS)] = shifted(c0)[:, :SHIFT_COLS]
        return carry

    n_full = (l - ext) // SHIFT_COLS + 1
    lax.fori_loop(0, n_full, shift_body, 0)
    for c0 in range(n_full * SHIFT_COLS, l, SHIFT_COLS):
        kt_ref[0, :, c0:c0 + SHIFT_COLS] = kt_in_ref[:, c0:c0 + SHIFT_COLS]
    kt_ref[1, :, l - SHIFT_COLS:l - GRID_W] = shifted(l - ext)[:, 2 * GRID_W:ext - GRID_W]

    top = lax.broadcasted_iota(jnp.int32, (1, pair), 1) < HEAD_DIM

    def window(r):
        if isinstance(r, int):
            rs = min(max(r - WIN_R // 2, 0), rows - WIN_R)
            return rs, r - rs, rs % 2
        rs = jnp.clip(r - WIN_R // 2, 0, rows - WIN_R)
        return rs, r - rs, lax.rem(rs, 2)

    def scores_stage(r, slot):
        rs, d, odd = window(r)
        q2 = qn_ref[pl.ds(_aligned(r * GRID_W, GRID_W), GRID_W), :]
        zero = jnp.zeros_like(q2)
        qq = jnp.concatenate([jnp.where(top, q2, zero), jnp.where(top, zero, q2)], axis=0)
        kcol = pl.ds(_aligned((rs - odd) * GRID_W, 2 * GRID_W), KEYS)
        s_ref[slot] = jnp.dot(qq, kt_ref[odd, :, kcol], preferred_element_type=F32) + bias_ref[d]

    def softmax_stage(slot):
        s = s_ref[slot]
        p_ref[slot] = jnp.exp(s - jnp.max(s, axis=-1, keepdims=True)).astype(BF16)

    def values_stage(r, slot):
        rs, _, _ = window(r)
        vrow = pl.ds(_aligned(rs * GRID_W, GRID_W), KEYS)
        ov = jnp.dot(p_ref[slot], vx_ref[vrow, :], preferred_element_type=F32)
        num = jnp.where(top, ov[:GRID_W, :pair], ov[GRID_W:, :pair])
        den = jnp.where(top, ov[:GRID_W, pair:], ov[GRID_W:, pair:])
        o_ref[pl.ds(_aligned(r * GRID_W, GRID_W), GRID_W), :] = (num / den).astype(BF16)

    n_groups = rows // ATT_ROWS

    def stages(t, par):
        static = isinstance(t, int)
        mine, other = par * ATT_ROWS, (1 - par) * ATT_ROWS
        for u in range(ATT_ROWS):
            if not static or 0 <= t - 2 < n_groups:
                values_stage((t - 2) * ATT_ROWS + u, mine + u)
        for u in range(ATT_ROWS):
            if not static or 0 <= t - 1 < n_groups:
                softmax_stage(other + u)
        for u in range(ATT_ROWS):
            if not static or 0 <= t < n_groups:
                scores_stage(t * ATT_ROWS + u, mine + u)

    stages(0, 0)
    stages(1, 1)

    def body(k, carry):
        stages(2 * k, 0)
        stages(2 * k + 1, 1)
        return carry

    lax.fori_loop(1, n_groups // 2, body, 0)
    stages(n_groups, 0)
    stages(n_groups + 1, 1)


def _attention(packed3, kt2, q_norm_g, bias_tab):
    b, l, _ = packed3.shape
    rows = l // GRID_W
    assert rows % (2 * ATT_ROWS) == 0 and l % SHIFT_COLS == 0
    pair = 2 * HEAD_DIM
    gq = jnp.tile(q_norm_g.reshape(1, HEAD_DIM), (1, 2))
    col = lambda off: (lambda i, j: (i, 0, off // pair + j))
    return pl.pallas_call(
        functools.partial(_attn_kernel, rows=rows),
        grid=(b, ATT_HEADS // 2),
        in_specs=[
            pl.BlockSpec((None, l, pair), col(OFF_Q)),
            pl.BlockSpec((pair, l), lambda i, j: (j, i)),
            pl.BlockSpec((None, l, pair), col(OFF_V)),
            pl.BlockSpec((1, pair), lambda i, j: (0, 0)),
            pl.BlockSpec((None, WIN_R, 2 * GRID_W, KEYS), lambda i, j: (j, 0, 0, 0)),
        ],
        out_specs=pl.BlockSpec((None, l, pair), lambda i, j: (i, 0, j)),
        out_shape=jax.ShapeDtypeStruct((b, l, ATT_W), BF16),
        scratch_shapes=[pltpu.VMEM((l, pair), BF16), pltpu.VMEM((2, pair, l), BF16),
                        pltpu.VMEM((l, 2 * pair), BF16), pltpu.VMEM((2 * ATT_ROWS, 2 * GRID_W, KEYS), F32),
                        pltpu.VMEM((2 * ATT_ROWS, 2 * GRID_W, KEYS), BF16)],
        compiler_params=_cparams(2),
        name="attention",
    )(packed3, kt2, packed3, gq, bias_tab)


CONV_TN = 512
CONV_ROWS = 128
CONV_UNROLL = 4
HALO = 16


def _roll_rows(x, offset, lo, n):
    if offset == 0:
        return x[lo:lo + n]
    return pltpu.roll(x, (-offset) % x.shape[0], 0)[lo:lo + n]


def _conv_kernel(x_ref, w_ref, b_ref, o_ref):
    l = x_ref.shape[0]
    n_steps = l // CONV_ROWS
    half = SSD_CONV // 2
    taps = [k for k in range(SSD_CONV) if k != half]
    ext_rows = CONV_ROWS + 2 * HALO
    ri = lax.broadcasted_iota(jnp.int32, (len(taps) * CONV_ROWS, ext_rows), 0)
    ci = lax.broadcasted_iota(jnp.int32, (len(taps) * CONV_ROWS, ext_rows), 1)
    src = ri % CONV_ROWS + HALO - half + ri // CONV_ROWS
    src = jnp.where(ri // CONV_ROWS >= half, src + 1, src)
    shift = jnp.where(ci == src, 1.0, 0.0).astype(BF16)

    def one(c):
        lo = pl.multiple_of(c * CONV_ROWS, CONV_ROWS)
        lo_before = pl.multiple_of(jnp.maximum(lo - HALO, 0), HALO)
        lo_after = pl.multiple_of(jnp.minimum(lo + CONV_ROWS, l - HALO), HALO)
        main = x_ref[pl.ds(lo, CONV_ROWS), :]
        before = x_ref[pl.ds(lo_before, HALO), :]
        after = x_ref[pl.ds(lo_after, HALO), :]
        zero = jnp.zeros_like(before)
        ext = jnp.concatenate([jnp.where(c > 0, before, zero), main,
                               jnp.where(c < n_steps - 1, after, zero)], axis=0)
        sh = jnp.dot(shift, ext, preferred_element_type=F32)
        acc = b_ref[...] + main.astype(F32) * w_ref[half:half + 1, :]
        for t, k in enumerate(taps):
            acc = acc + sh[t * CONV_ROWS:(t + 1) * CONV_ROWS] * w_ref[k:k + 1, :]
        o_ref[pl.ds(lo, CONV_ROWS), :] = _silu(acc).astype(BF16)

    def body(t, carry):
        for u in range(CONV_UNROLL):
            one(CONV_UNROLL * t + u)
        return carry

    lax.fori_loop(0, n_steps // CONV_UNROLL, body, 0)


def _ssd_conv(packed3, conv_w, conv_b):
    b, l, _ = packed3.shape
    assert l % (CONV_UNROLL * CONV_ROWS) == 0
    c0 = OFF_XBC // CONV_TN
    return pl.pallas_call(
        _conv_kernel,
        grid=(b, CONV_DIM // CONV_TN),
        in_specs=[
            pl.BlockSpec((None, l, CONV_TN), lambda i, j: (i, 0, c0 + j)),
            pl.BlockSpec((SSD_CONV, CONV_TN), lambda i, j: (0, j)),
            pl.BlockSpec((1, CONV_TN), lambda i, j: (0, j)),
        ],
        out_specs=pl.BlockSpec((None, l, CONV_TN), lambda i, j: (i, 0, j)),
        out_shape=jax.ShapeDtypeStruct((b, l, CONV_DIM), BF16),
        compiler_params=_cparams(2),
        name="ssd_conv",
    )(packed3, conv_w, conv_b)


GW = SSD_HPG * SSD_HEAD_DIM


def _split3(v):
    hi = v.astype(BF16)
    r1 = v - hi.astype(F32)
    mid = r1.astype(BF16)
    lo = (r1 - mid.astype(F32)).astype(BF16)
    return hi, mid, lo


ND = 2 * SSD_HPG
NDT = 2 * SSD_HEADS
PIECE_W = 6 * NDT
SPREAD_W = SSD_HPG * CHUNK + GW


def _group_major_perm():
    return [d * SSD_HEADS + g * SSD_HPG + r for g in range(SSD_GROUPS) for d in range(2) for r in range(SSD_HPG)]


def _piece_layout():
    import numpy as np
    m = np.zeros((PIECE_W, SSD_GROUPS * CHUNK), np.float32)
    for q in range(6):
        for g in range(SSD_GROUPS):
            for k in range(ND):
                m[q * NDT + ND * g + k, CHUNK * g + ND * q + k] = 1.0
    return jnp.asarray(m, BF16)


def _spread_layout():
    import numpy as np
    m = np.zeros((2, CHUNK, SPREAD_W), np.float32)
    for d in range(2):
        for q in range(6):
            for r in range(SSD_HPG):
                row = ND * q + SSD_HPG * d + r
                if q < 3:
                    m[d, row, CHUNK * r:CHUNK * (r + 1)] = 1.0
                else:
                    m[d, row, SSD_HPG * CHUNK + SSD_HEAD_DIM * r:SSD_HPG * CHUNK + SSD_HEAD_DIM * (r + 1)] = 1.0
    return jnp.asarray(m, BF16)


def _prep_kernel(dt_ref, alog_ref, lay_ref, pcs_ref, acsr_ref):
    neg_a = -jnp.exp(alog_ref[...]) * LOG2E
    ii = lax.broadcasted_iota(jnp.int32, (CHUNK, CHUNK), 0)
    jj = lax.broadcasted_iota(jnp.int32, (CHUNK, CHUNK), 1)
    tri_f = jnp.where(jj <= ii, 1.0, 0.0).astype(BF16)
    tri_b = jnp.where(jj >= ii, 1.0, 0.0).astype(BF16)
    col = lax.broadcasted_iota(jnp.int32, (1, 3 * NDT), 1)
    is_fwd = (col % ND) < SSD_HPG
    er = lax.broadcasted_iota(jnp.int32, (NDT, NDT), 0)
    ec = lax.broadcasted_iota(jnp.int32, (NDT, NDT), 1)
    eye = jnp.where(er == ec, 1.0, 0.0).astype(BF16)
    for c in range(PREP_CHUNKS):
        rows = slice(c * CHUNK, (c + 1) * CHUNK)
        dt = dt_ref[rows, :]
        a3 = jnp.concatenate(_split3(dt * neg_a), axis=-1)
        s3 = jnp.where(is_fwd, jnp.dot(tri_f, a3, preferred_element_type=F32),
                       jnp.dot(tri_b, a3, preferred_element_type=F32))
        acs = s3[:, 0:NDT] + s3[:, NDT:2 * NDT] + s3[:, 2 * NDT:3 * NDT]
        acs3 = _split3(acs)
        pieces = jnp.concatenate(acs3 + _split3(dt), axis=-1)
        pcs_ref[rows, :] = jnp.dot(pieces, lay_ref[...], preferred_element_type=F32).astype(BF16)
        tr = None
        for p in acs3:
            d = lax.dot_general(eye, p, (((1,), (1,)), ((), ())), preferred_element_type=F32)
            tr = d if tr is None else tr + d
        acsr_ref[c] = tr


PREP_CHUNKS = 4


def _ssd_prep(dt3, a_log):
    b, l, _ = dt3.shape
    nc = l // CHUNK
    assert nc % PREP_CHUNKS == 0
    rows = PREP_CHUNKS * CHUNK
    return pl.pallas_call(
        _prep_kernel,
        grid=(b, nc // PREP_CHUNKS),
        in_specs=[
            pl.BlockSpec((None, rows, NDT), lambda i, c: (i, c, 0)),
            pl.BlockSpec((1, NDT), lambda i, c: (0, 0)),
            pl.BlockSpec((PIECE_W, SSD_GROUPS * CHUNK), lambda i, c: (0, 0)),
        ],
        out_specs=[
            pl.BlockSpec((None, rows, SSD_GROUPS * CHUNK), lambda i, c: (i, c, 0)),
            pl.BlockSpec((None, PREP_CHUNKS, NDT, CHUNK), lambda i, c: (i, c, 0, 0)),
        ],
        out_shape=[
            jax.ShapeDtypeStruct((b, l, SSD_GROUPS * CHUNK), BF16),
            jax.ShapeDtypeStruct((b, nc, NDT, CHUNK), F32),
        ],
        compiler_params=_cparams(2),
        name="ssd_prep",
    )(dt3, a_log, _piece_layout())


def _ssd_kernel(x_ref, b_ref, c_ref, z_ref, pcs_ref, acsr_ref, lay_ref, dskip_ref, ng_ref, o_ref,
                yp_ref, st_ref, g_ref, xdt_ref, xw_ref, ea_ref, et_ref, *, nc):
    ii = lax.broadcasted_iota(jnp.int32, (CHUNK, CHUNK), 0)
    jj = lax.broadcasted_iota(jnp.int32, (CHUNK, CHUNK), 1)
    keep_f = jj <= ii
    keep_b = jj >= ii
    low_half = lax.broadcasted_iota(jnp.int32, (1, CHUNK), 1) < SSD_HEAD_DIM
    head_of_lane = lax.broadcasted_iota(jnp.int32, (1, GW), 1) // SSD_HEAD_DIM
    st_ref[...] = jnp.zeros_like(st_ref)

    def chunk_rows(c):
        return pl.ds(_aligned(c * CHUNK, CHUNK), CHUNK)

    def decay_stage(c, d, par):
        slot = 2 * par + d
        keep = keep_b if d else keep_f
        rows = chunk_rows(c)
        sp = jnp.dot(pcs_ref[rows, :], lay_ref[d], preferred_element_type=F32)
        acs_w = [sp[:, r * CHUNK:(r + 1) * CHUNK] for r in range(SSD_HPG)]
        dt_e = sp[:, SSD_HPG * CHUNK:]
        acs_e = jnp.concatenate([jnp.where(low_half, acs_w[0], acs_w[1]),
                                 jnp.where(low_half, acs_w[2], acs_w[3])], axis=-1)
        acs_r = acsr_ref[c]
        xdt = x_ref[rows, :].astype(F32) * dt_e
        scores = lax.dot_general(c_ref[rows, :], b_ref[rows, :], (((1,), (1,)), ((), ())),
                                 preferred_element_type=F32)
        for r in range(SSD_HPG):
            seg = acs_w[r] - acs_r[SSD_HPG * d + r:SSD_HPG * d + r + 1, :]
            g_ref[slot, :, r * CHUNK:(r + 1) * CHUNK] = (scores * jnp.exp2(jnp.where(keep, seg, NEG))).astype(BF16)
        total = acs_e[0:1, :] if d else acs_e[CHUNK - 1:CHUNK, :]
        xdt_ref[slot] = xdt.astype(BF16)
        xw_ref[slot] = (xdt * jnp.exp2(total - acs_e)).astype(BF16)
        ea_ref[slot] = jnp.exp2(acs_e)
        et_ref[slot] = jnp.exp2(total)

    def state_stage(c, d, par):
        slot = 2 * par + d
        rows = chunk_rows(c)
        xdt_b = xdt_ref[slot]
        zero = jnp.zeros_like(xdt_b)
        xbd = jnp.concatenate([jnp.where(head_of_lane == r, xdt_b, zero) for r in range(SSD_HPG)], axis=0)
        cm = c_ref[rows, :]
        st = st_ref[d]
        y = (jnp.dot(g_ref[slot], xbd, preferred_element_type=F32)
             + jnp.dot(cm, st.astype(BF16), preferred_element_type=F32) * ea_ref[slot])
        st_ref[d] = st * et_ref[slot] + lax.dot_general(
            b_ref[rows, :], xw_ref[slot], (((0,), (0,)), ((), ())), preferred_element_type=F32)
        return rows, y

    def first_visit(c, d, par):
        rows, y = state_stage(c, d, par)
        yp_ref[rows, :] = y

    def second_visit(c, d, par):
        rows, y = state_stage(c, d, par)
        xf = x_ref[rows, :].astype(F32)
        yt = (yp_ref[rows, :] + y + xf * dskip_ref[...]) * _silu(z_ref[rows, :].astype(F32))
        o_ref[rows, :] = _rms(yt, ng_ref[...]).astype(BF16)

    def step(t, par, visit):
        visit(t, 0, par)
        visit(nc - 1 - t, 1, par)
        decay_stage(t + 1, 0, 1 - par)
        decay_stage(nc - 2 - t, 1, 1 - par)

    def pair_of_steps(visit, t0):
        def body(k, carry):
            step(t0 + 2 * k, 0, visit)
            step(t0 + 2 * k + 1, 1, visit)
            return carry
        return body

    decay_stage(0, 0, 0)
    decay_stage(nc - 1, 1, 0)
    lax.fori_loop(0, nc // 4, pair_of_steps(first_visit, 0), 0)
    lax.fori_loop(0, nc // 4 - 1, pair_of_steps(second_visit, nc // 2), 0)
    step(nc - 2, 0, second_visit)
    second_visit(nc - 1, 0, 1)
    second_visit(0, 1, 1)


def _ssd(xbc3, packed3, pcs3, acsr4, d_skip_e, norm_g):
    b, l, _ = xbc3.shape
    nc = l // CHUNK
    assert nc % 4 == 0
    b0 = SSD_INNER // SSD_STATE
    seq = lambda w, j0: pl.BlockSpec((None, l, w), lambda i, g: (i, 0, j0 + g))
    return pl.pallas_call(
        functools.partial(_ssd_kernel, nc=nc),
        grid=(b, SSD_GROUPS),
        in_specs=[
            seq(GW, 0), seq(SSD_STATE, b0), seq(SSD_STATE, b0 + SSD_GROUPS), seq(GW, OFF_Z // GW),
            seq(CHUNK, 0),
            pl.BlockSpec((None, nc, ND, CHUNK), lambda i, g: (i, 0, g, 0)),
            pl.BlockSpec((2, CHUNK, SPREAD_W), lambda i, g: (0, 0, 0)),
            pl.BlockSpec((1, GW), lambda i, g: (0, g)),
            pl.BlockSpec((1, GW), lambda i, g: (0, g)),
        ],
        out_specs=seq(GW, 0),
        out_shape=jax.ShapeDtypeStruct((b, l, SSD_INNER), BF16),
        scratch_shapes=[
            pltpu.VMEM((l, GW), F32),
            pltpu.VMEM((2, SSD_STATE, GW), F32),
            pltpu.VMEM((4, CHUNK, SSD_HPG * CHUNK), BF16),
            pltpu.VMEM((4, CHUNK, GW), BF16),
            pltpu.VMEM((4, CHUNK, GW), BF16),
            pltpu.VMEM((4, CHUNK, GW), F32),
            pltpu.VMEM((4, 1, GW), F32),
        ],
        compiler_params=_cparams(2),
        name="ssd",
    )(xbc3, xbc3, xbc3, packed3, pcs3, acsr4, _spread_layout(), d_skip_e, norm_g)


MERGE_TM = 512


def _merge_kernel(att_ref, y_ref, ga_ref, gs_ref, x_ref, ba_ref, bs_ref, wa_ref, ws_ref, wo_ref, o_ref):
    m1 = jnp.dot(att_ref[...], wa_ref[...], preferred_element_type=F32)
    m2 = jnp.dot(y_ref[...], ws_ref[...], preferred_element_type=F32)
    merged = (jax.nn.sigmoid(ga_ref[...].astype(F32) + ba_ref[...]) * m1
              + jax.nn.sigmoid(gs_ref[...].astype(F32) + bs_ref[...]) * m2)
    o_ref[...] = x_ref[...] + jnp.dot(merged.astype(BF16), wo_ref[...], preferred_element_type=F32)


def _merge(att2, y2, packed2, x2, b_gate, wa, ws, wo):
    m = x2.shape[0]
    tm = min(MERGE_TM, m)
    row = lambda w, j=0: pl.BlockSpec((tm, w), lambda i: (i, j))
    full = lambda r, c, j=0: pl.BlockSpec((r, c), lambda i: (0, j))
    return pl.pallas_call(
        _merge_kernel,
        grid=(m // tm,),
        in_specs=[
            row(ATT_W), row(SSD_INNER), row(D_MODEL, OFF_GA // D_MODEL), row(D_MODEL, OFF_GS // D_MODEL),
            row(D_MODEL), full(1, D_MODEL, 0), full(1, D_MODEL, 1),
            full(ATT_W, D_MODEL), full(SSD_INNER, D_MODEL), full(D_MODEL, D_MODEL),
        ],
        out_specs=row(D_MODEL),
        out_shape=jax.ShapeDtypeStruct((m, D_MODEL), F32),
        compiler_params=_cparams(1),
        name="merge",
    )(att2, y2, packed2, packed2, x2, b_gate, b_gate, wa, ws, wo)


FFN_TM = 512
FFN_TF = 1408
FFN_HALO = 8


def _ffn_kernel(prev_ref, x_ref, next_ref, g_ref, wa_ref, wg_ref, cwa_ref, cwg_ref, cba_ref, cbg_ref,
                wd_ref, o_ref, h_ref, acc_ref):
    m = pl.program_id(1)
    f = pl.program_id(2)
    tm = x_ref.shape[0]

    @pl.when(f == 0)
    def _():
        prev = jnp.where(m > 0, prev_ref[...], 0.0)
        nxt = jnp.where(m < pl.num_programs(1) - 1, next_ref[...], 0.0)
        ext = jnp.concatenate([prev, x_ref[...], nxt], axis=0)
        h_ref[...] = _rms(ext, g_ref[...]).astype(BF16)
        acc_ref[...] = jnp.zeros_like(acc_ref)

    h = h_ref[...]

    def conv(w_ref, cw_ref, cb_ref):
        u = jnp.dot(h, w_ref[...], preferred_element_type=F32)
        out = jnp.broadcast_to(cb_ref[...], (tm, u.shape[1]))
        for k in range(FFN_CONV):
            out = out + _roll_rows(u, k - FFN_CONV // 2, FFN_HALO, tm) * cw_ref[k:k + 1, :]
        return out

    act = conv(wa_ref, cwa_ref, cba_ref) * _silu(conv(wg_ref, cwg_ref, cbg_ref))
    acc_ref[...] += jnp.dot(act.astype(BF16), wd_ref[...], preferred_element_type=F32)

    @pl.when(f == pl.num_programs(2) - 1)
    def _():
        o_ref[...] = x_ref[...] + acc_ref[...]


def _ffn(x3, g_ffn, w_up, conv_w, conv_b, w_down):
    b, l, _ = x3.shape
    tm = min(FFN_TM, l)
    nf = D_FF // FFN_TF
    hb = tm // FFN_HALO
    wcol = lambda r, j0: pl.BlockSpec((r, FFN_TF), lambda i, m, f: (0, j0 + f))
    return pl.pallas_call(
        _ffn_kernel,
        grid=(b, l // tm, nf),
        in_specs=[
            pl.BlockSpec((None, FFN_HALO, D_MODEL), lambda i, m, f: (i, jnp.maximum(m * hb - 1, 0), 0)),
            pl.BlockSpec((None, tm, D_MODEL), lambda i, m, f: (i, m, 0)),
            pl.BlockSpec((None, FFN_HALO, D_MODEL),
                         lambda i, m, f: (i, jnp.minimum((m + 1) * hb, l // FFN_HALO - 1), 0)),
            pl.BlockSpec((1, D_MODEL), lambda i, m, f: (0, 0)),
            wcol(D_MODEL, 0), wcol(D_MODEL, nf),
            wcol(FFN_CONV, 0), wcol(FFN_CONV, nf),
            wcol(1, 0), wcol(1, nf),
            pl.BlockSpec((FFN_TF, D_MODEL), lambda i, m, f: (f, 0)),
        ],
        out_specs=pl.BlockSpec((None, tm, D_MODEL), lambda i, m, f: (i, m, 0)),
        out_shape=jax.ShapeDtypeStruct((b, l, D_MODEL), F32),
        scratch_shapes=[pltpu.VMEM((tm + 2 * FFN_HALO, D_MODEL), BF16), pltpu.VMEM((tm, D_MODEL), F32)],
        compiler_params=_cparams(3),
        name="ffn",
    )(x3, x3, x3, g_ffn, w_up, w_up, conv_w, conv_w, conv_b, conv_b, w_down)


def _layer(x, bias_tab, p):
    b, l, _ = x.shape
    x2 = x.reshape(b * l, D_MODEL)
    packed2, dt2, kt2 = _proj(x2, p["g_mix"], p["w_main"], p["w_dt"], p["dt_bias"], p["w_kt"], p["k_norm_g"])
    packed3 = packed2.reshape(b, l, PROJ_W)
    att = _attention(packed3, kt2, p["q_norm_g"], bias_tab)
    xbc = _ssd_conv(packed3, p["ssd_conv_w"], p["ssd_conv_b"])
    pcs, acsr = _ssd_prep(dt2.reshape(b, l, NDT), p["a_log"])
    y = _ssd(xbc, packed3, pcs, acsr, p["d_skip_e"], p["ssd_norm_g"])
    x1 = _merge(att.reshape(b * l, ATT_W), y.reshape(b * l, SSD_INNER), packed2, x2,
                p["b_gate"], p["w_att_out"], p["w_ssd_out"], p["w_o"])
    return _ffn(x1.reshape(b, l, D_MODEL), p["g_ffn"], p["w_up"], p["ffn_conv_w"], p["ffn_conv_b"],
                p["w_down"])


def kernel(x_prompt, x_sample, g_mix, w_in, b_gate, q_norm_g, k_norm_g, rel_bias, ssd_conv_w, ssd_conv_b,
           dt_bias_f, dt_bias_b, A_log_f, A_log_b, D_skip, ssd_norm_g, w_att_out, w_ssd_out, w_o, g_ffn,
           w_up, ffn_conv_w, ffn_conv_b, w_down):
    depth = g_mix.shape[0]
    xs = [x_prompt, x_sample]
    perm = jnp.asarray(_group_major_perm(), jnp.int32)
    for layer in range(depth):
        w = w_in[layer]
        p = {
            "g_mix": g_mix[layer][None, :],
            "w_main": jnp.concatenate([w[:, :ATT_W], w[:, 2 * ATT_W:DT_COL], w[:, DT_COL + 2 * SSD_HEADS:]],
                                      axis=1).astype(BF16),
            "w_kt": w[:, ATT_W:2 * ATT_W].T.astype(BF16),
            "w_dt": w[:, DT_COL:DT_COL + 2 * SSD_HEADS][:, perm].astype(BF16),
            "dt_bias": jnp.concatenate([dt_bias_f[layer], dt_bias_b[layer]])[perm][None, :],
            "q_norm_g": q_norm_g[layer],
            "k_norm_g": k_norm_g[layer],
            "ssd_conv_w": ssd_conv_w[layer],
            "ssd_conv_b": ssd_conv_b[layer][None, :],
            "a_log": jnp.concatenate([A_log_f[layer], A_log_b[layer]])[perm][None, :],
            "d_skip_e": jnp.repeat(D_skip[layer], SSD_HEAD_DIM)[None, :],
            "ssd_norm_g": ssd_norm_g[layer][None, :],
            "b_gate": b_gate[layer][None, :],
            "w_att_out": w_att_out[layer].astype(BF16),
            "w_ssd_out": w_ssd_out[layer].astype(BF16),
            "w_o": w_o[layer].astype(BF16),
            "g_ffn": g_ffn[layer][None, :],
            "w_up": w_up[layer].astype(BF16),
            "ffn_conv_w": ffn_conv_w[layer],
            "ffn_conv_b": ffn_conv_b[layer][None, :],
            "w_down": w_down[layer].astype(BF16),
        }
        bias_tab = _bias_table(rel_bias[layer])
        xs = [_layer(x, bias_tab, p) for x in xs]
    return tuple(xs)
```

```python
import functools

import jax
import jax.numpy as jnp
from jax import lax
from jax.experimental import pallas as pl
from jax.experimental.pallas import tpu as pltpu

F32 = jnp.float32
BF16 = jnp.bfloat16
HIGHEST = lax.Precision.HIGHEST

D_MODEL = 1024
GRID_W = 64
WIN_R = 8
WIN_C = 16
ATT_HEADS = 16
HEAD_DIM = 64
ATT_W = ATT_HEADS * HEAD_DIM
SSD_INNER = 2 * D_MODEL
SSD_HEAD_DIM = 64
SSD_HEADS = SSD_INNER // SSD_HEAD_DIM
SSD_GROUPS = 8
SSD_HPG = SSD_HEADS // SSD_GROUPS
SSD_STATE = 128
SSD_CONV = 5
CONV_DIM = SSD_INNER + 2 * SSD_GROUPS * SSD_STATE
CHUNK = 128
D_FF = 2816
FFN_CONV = 3
EPS = 1e-6

OFF_Q = 0
OFF_V = OFF_Q + ATT_W
OFF_Z = OFF_V + ATT_W
OFF_XBC = OFF_Z + SSD_INNER
OFF_GA = OFF_XBC + CONV_DIM
OFF_GS = OFF_GA + D_MODEL
PROJ_W = OFF_GS + D_MODEL
DT_COL = 3 * ATT_W + SSD_INNER + CONV_DIM

NEG = -1e30
LOG2E = 1.4426950408889634
VMEM_LIMIT = 56 * 1024 * 1024


def _cparams(n_axes):
    return pltpu.CompilerParams(
        dimension_semantics=("arbitrary",) * n_axes, vmem_limit_bytes=VMEM_LIMIT)


def _rms(x, g):
    return x * lax.rsqrt(jnp.mean(x * x, axis=-1, keepdims=True) + EPS) * g


def _silu(x):
    return x * jax.nn.sigmoid(x)


PROJ_TM = 1024
PROJ_TN = 2048


PROJ_NT = PROJ_W // PROJ_TN
KT_COLS = 256


def _proj_kernel(x_ref, g_ref, w_ref, wdt_ref, dtb_ref, wkt_ref, gk_ref, o_ref, dt_ref, kt_ref, h_ref):
    j = pl.program_id(1)

    @pl.when(j == 0)
    def _():
        h = _rms(x_ref[...], g_ref[...]).astype(BF16)
        h_ref[...] = h
        raw = jnp.dot(h, wdt_ref[...], preferred_element_type=F32)
        dt_ref[...] = jax.nn.softplus(raw + dtb_ref[...])

    @pl.when(j < PROJ_NT)
    def _():
        o_ref[...] = jnp.dot(h_ref[...], w_ref[...], preferred_element_type=F32).astype(BF16)

    @pl.when(j == PROJ_NT)
    def _():
        gk = gk_ref[...].reshape(1, HEAD_DIM, 1)
        for c in range(h_ref.shape[0] // KT_COLS):
            cols = slice(c * KT_COLS, (c + 1) * KT_COLS)
            kt = lax.dot_general(wkt_ref[...], h_ref[cols, :], (((1,), (1,)), ((), ())),
                                 preferred_element_type=F32)
            kt = kt.reshape(ATT_HEADS, HEAD_DIM, KT_COLS)
            ms = jnp.mean(kt * kt, axis=1, keepdims=True)
            kt_ref[:, cols] = (kt * lax.rsqrt(ms + EPS) * gk).reshape(ATT_W, KT_COLS).astype(BF16)


def _proj(x2, g_mix, w_main, w_dt, dt_bias, w_kt, k_norm_g):
    m = x2.shape[0]
    tm = min(PROJ_TM, m)
    last = PROJ_NT - 1
    return pl.pallas_call(
        _proj_kernel,
        grid=(m // tm, PROJ_NT + 1),
        in_specs=[
            pl.BlockSpec((tm, D_MODEL), lambda i, j: (i, 0)),
            pl.BlockSpec((1, D_MODEL), lambda i, j: (0, 0)),
            pl.BlockSpec((D_MODEL, PROJ_TN), lambda i, j: (0, jnp.minimum(j, last))),
            pl.BlockSpec((D_MODEL, 2 * SSD_HEADS), lambda i, j: (0, 0)),
            pl.BlockSpec((1, 2 * SSD_HEADS), lambda i, j: (0, 0)),
            pl.BlockSpec((ATT_W, D_MODEL), lambda i, j: (0, 0)),
            pl.BlockSpec((HEAD_DIM, 1), lambda i, j: (0, 0)),
        ],
        out_specs=[
            pl.BlockSpec((tm, PROJ_TN), lambda i, j: (i, jnp.minimum(j, last))),
            pl.BlockSpec((tm, 2 * SSD_HEADS), lambda i, j: (i, 0)),
            pl.BlockSpec((ATT_W, tm), lambda i, j: (0, i)),
        ],
        out_shape=[
            jax.ShapeDtypeStruct((m, PROJ_W), BF16),
            jax.ShapeDtypeStruct((m, 2 * SSD_HEADS), F32),
            jax.ShapeDtypeStruct((ATT_W, m), BF16),
        ],
        scratch_shapes=[pltpu.VMEM((tm, D_MODEL), BF16)],
        compiler_params=_cparams(2),
        name="proj",
    )(x2, g_mix, w_main, w_dt, dt_bias, w_kt, k_norm_g.reshape(HEAD_DIM, 1))


N_ROFF = 2 * WIN_R - 1
N_COFF = 2 * WIN_C - 1
KEYS = WIN_R * GRID_W


def _bias_kernel(rb_ref, o_ref):
    h = pl.program_id(0)
    c = lax.broadcasted_iota(jnp.int32, (GRID_W, GRID_W), 0)
    cp = lax.broadcasted_iota(jnp.int32, (GRID_W, GRID_W), 1)
    cs = jnp.clip(c - WIN_C // 2, 0, GRID_W - WIN_C)
    inside = jnp.where(cp >= cs, jnp.where(cp < cs + WIN_C, 1, 0), 0)
    idx = jnp.clip(cp - c + WIN_C - 1, 0, N_COFF - 1)
    tiles = []
    for ro in range(N_ROFF):
        base = (h * N_ROFF + ro) * N_COFF

        def body(k, acc, base=base):
            return jnp.where(idx == k, rb_ref[base + k], acc)

        t = lax.fori_loop(0, N_COFF, body, jnp.zeros((GRID_W, GRID_W), F32))
        tiles.append(jnp.where(inside == 1, t, NEG))
    for d in range(WIN_R):
        for i in range(WIN_R):
            o_ref[0, d, :, i * GRID_W:(i + 1) * GRID_W] = tiles[i - d + WIN_R - 1]


def _bias_table(rel_bias):
    return pl.pallas_call(
        _bias_kernel,
        grid=(ATT_HEADS,),
        in_specs=[pl.BlockSpec(memory_space=pltpu.SMEM)],
        out_specs=pl.BlockSpec((1, WIN_R, GRID_W, KEYS), lambda h: (h // 2, 0, h % 2, 0)),
        out_shape=jax.ShapeDtypeStruct((ATT_HEADS // 2, WIN_R, 2 * GRID_W, KEYS), F32),
        compiler_params=_cparams(1),
        name="bias_table",
    )(rel_bias.reshape(-1))


NORM_ROWS = 512


ATT_ROWS = 8
SHIFT_COLS = 256


def _aligned(x, m):
    return x if isinstance(x, int) else pl.multiple_of(x, m)


def _attn_kernel(q_ref, kt_in_ref, v_ref, gq_ref, bias_ref, o_ref, qn_ref, kt_ref, vx_ref, s_ref, p_ref, *, rows):
    l = rows * GRID_W
    pair = 2 * HEAD_DIM
    lane = lax.broadcasted_iota(jnp.int32, (2 * pair, pair), 0) % pair // HEAD_DIM
    lane_t = lax.broadcasted_iota(jnp.int32, (2 * pair, pair), 1) // HEAD_DIM
    seg = jnp.where(lane == lane_t, 1.0 / HEAD_DIM, 0.0).astype(BF16)

    def norm_body(i, carry):
        sl = pl.ds(pl.multiple_of(i * NORM_ROWS, NORM_ROWS), NORM_ROWS)
        x = q_ref[sl, :].astype(F32)
        sq = x * x
        hi = sq.astype(BF16)
        lo = (sq - hi.astype(F32)).astype(BF16)
        ms = jnp.dot(jnp.concatenate([hi, lo], axis=-1), seg, preferred_element_type=F32)
        qn_ref[sl, :] = (x * lax.rsqrt(ms + EPS) * gq_ref[...] * (HEAD_DIM ** -0.5)).astype(BF16)
        vx_ref[sl, 0:pair] = v_ref[sl, :]
        vx_ref[sl, pair:2 * pair] = jnp.ones((NORM_ROWS, pair), BF16)
        return carry

    lax.fori_loop(0, l // NORM_ROWS, norm_body, 0)

    ext = SHIFT_COLS + 2 * GRID_W

    def shifted(c0):
        words = pltpu.bitcast(kt_in_ref[:, pl.ds(c0, ext)], jnp.uint32)
        return pltpu.bitcast(pltpu.roll(words, ext - GRID_W, 1), BF16)

    def shift_body(i, carry):
        c0 = pl.multiple_of(i * SHIFT_COLS, SHIFT_COLS)
        kt_ref[0, :, pl.ds(c0, SHIFT_COLS)] = kt_in_ref[:, pl.ds(c0, SHIFT_COLS)]
        kt_ref[1, :, pl.ds(c0, SHIFT_COLS)] = shifted(c0)[:, :SHIFT_COLS]
        return carry

    n_full = (l - ext) // SHIFT_COLS + 1
    lax.fori_loop(0, n_full, shift_body, 0)
    for c0 in range(n_full * SHIFT_COLS, l, SHIFT_COLS):
        kt_ref[0, :, c0:c0 + SHIFT_COLS] = kt_in_ref[:, c0:c0 + SHIFT_COLS]
    kt_ref[1, :, l - SHIFT_COLS:l - GRID_W] = shifted(l - ext)[:, 2 * GRID_W:ext - GRID_W]

    top = lax.broadcasted_iota(jnp.int32, (1, pair), 1) < HEAD_DIM

    def window(r):
        if isinstance(r, int):
            rs = min(max(r - WIN_R // 2, 0), rows - WIN_R)
            return rs, r - rs, rs % 2
        rs = jnp.clip(r - WIN_R // 2, 0, rows - WIN_R)
        return rs, r - rs, lax.rem(rs, 2)

    def scores_stage(r, slot):
        rs, d, odd = window(r)
        q2 = qn_ref[pl.ds(_aligned(r * GRID_W, GRID_W), GRID_W), :]
        zero = jnp.zeros_like(q2)
        qq = jnp.concatenate([jnp.where(top, q2, zero), jnp.where(top, zero, q2)], axis=0)
        kcol = pl.ds(_aligned((rs - odd) * GRID_W, 2 * GRID_W), KEYS)
        s_ref[slot] = jnp.dot(qq, kt_ref[odd, :, kcol], preferred_element_type=F32) + bias_ref[d]

    def softmax_stage(slot):
        s = s_ref[slot]
        p_ref[slot] = jnp.exp(s - jnp.max(s, axis=-1, keepdims=True)).astype(BF16)

    def values_stage(r, slot):
        rs, _, _ = window(r)
        vrow = pl.ds(_aligned(rs * GRID_W, GRID_W), KEYS)
        ov = jnp.dot(p_ref[slot], vx_ref[vrow, :], preferred_element_type=F32)
        num = jnp.where(top, ov[:GRID_W, :pair], ov[GRID_W:, :pair])
        den = jnp.where(top, ov[:GRID_W, pair:], ov[GRID_W:, pair:])
        o_ref[pl.ds(_aligned(r * GRID_W, GRID_W), GRID_W), :] = (num / den).astype(BF16)

    n_groups = rows // ATT_ROWS

    def stages(t, par):
        static = isinstance(t, int)
        mine, other = par * ATT_ROWS, (1 - par) * ATT_ROWS
        for u in range(ATT_ROWS):
            if not static or 0 <= t - 2 < n_groups:
                values_stage((t - 2) * ATT_ROWS + u, mine + u)
        for u in range(ATT_ROWS):
            if not static or 0 <= t - 1 < n_groups:
                softmax_stage(other + u)
        for u in range(ATT_ROWS):
            if not static or 0 <= t < n_groups:
                scores_stage(t * ATT_ROWS + u, mine + u)

    stages(0, 0)
    stages(1, 1)

    def body(k, carry):
        stages(2 * k, 0)
        stages(2 * k + 1, 1)
        return carry

    lax.fori_loop(1, n_groups // 2, body, 0)
    stages(n_groups, 0)
    stages(n_groups + 1, 1)


def _attention(packed3, kt2, q_norm_g, bias_tab):
    b, l, _ = packed3.shape
    rows = l // GRID_W
    assert rows % (2 * ATT_ROWS) == 0 and l % SHIFT_COLS == 0
    pair = 2 * HEAD_DIM
    gq = jnp.tile(q_norm_g.reshape(1, HEAD_DIM), (1, 2))
    col = lambda off: (lambda i, j: (i, 0, off // pair + j))
    return pl.pallas_call(
        functools.partial(_attn_kernel, rows=rows),
        grid=(b, ATT_HEADS // 2),
        in_specs=[
            pl.BlockSpec((None, l, pair), col(OFF_Q)),
            pl.BlockSpec((pair, l), lambda i, j: (j, i)),
            pl.BlockSpec((None, l, pair), col(OFF_V)),
            pl.BlockSpec((1, pair), lambda i, j: (0, 0)),
            pl.BlockSpec((None, WIN_R, 2 * GRID_W, KEYS), lambda i, j: (j, 0, 0, 0)),
        ],
        out_specs=pl.BlockSpec((None, l, pair), lambda i, j: (i, 0, j)),
        out_shape=jax.ShapeDtypeStruct((b, l, ATT_W), BF16),
        scratch_shapes=[pltpu.VMEM((l, pair), BF16), pltpu.VMEM((2, pair, l), BF16),
                        pltpu.VMEM((l, 2 * pair), BF16), pltpu.VMEM((2 * ATT_ROWS, 2 * GRID_W, KEYS), F32),
                        pltpu.VMEM((2 * ATT_ROWS, 2 * GRID_W, KEYS), BF16)],
        compiler_params=_cparams(2),
        name="attention",
    )(packed3, kt2, packed3, gq, bias_tab)


CONV_TN = 512
CONV_ROWS = 128
CONV_UNROLL = 4
HALO = 16


def _roll_rows(x, offset, lo, n):
    if offset == 0:
        return x[lo:lo + n]
    return pltpu.roll(x, (-offset) % x.shape[0], 0)[lo:lo + n]


CONV_TAPS = [k for k in range(SSD_CONV) if k != SSD_CONV // 2]


def _conv_shift_matrix():
    half = SSD_CONV // 2
    ext_rows = CONV_ROWS + 2 * HALO
    ri = lax.broadcasted_iota(jnp.int32, (len(CONV_TAPS) * CONV_ROWS, ext_rows), 0)
    ci = lax.broadcasted_iota(jnp.int32, (len(CONV_TAPS) * CONV_ROWS, ext_rows), 1)
    src = ri % CONV_ROWS + HALO - half + ri // CONV_ROWS
    src = jnp.where(ri // CONV_ROWS >= half, src + 1, src)
    return jnp.where(ci == src, 1.0, 0.0).astype(BF16)


def _conv_step(x_ref, w_ref, b_ref, o_ref, shift, c):
    l = x_ref.shape[0]
    half = SSD_CONV // 2
    lo = c * CONV_ROWS
    if isinstance(c, int):
        lo_before, lo_after = max(lo - HALO, 0), min(lo + CONV_ROWS, l - HALO)
    else:
        lo_before, lo_after = jnp.maximum(lo - HALO, 0), jnp.minimum(lo + CONV_ROWS, l - HALO)
    main = x_ref[pl.ds(_aligned(lo, CONV_ROWS), CONV_ROWS), :]
    before = x_ref[pl.ds(_aligned(lo_before, HALO), HALO), :]
    after = x_ref[pl.ds(_aligned(lo_after, HALO), HALO), :]
    zero = jnp.zeros_like(before)
    ext = jnp.concatenate([jnp.where(c > 0, before, zero), main,
                           jnp.where(c < l // CONV_ROWS - 1, after, zero)], axis=0)
    sh = jnp.dot(shift, ext, preferred_element_type=F32)
    acc = b_ref[...] + main.astype(F32) * w_ref[half:half + 1, :]
    for t, k in enumerate(CONV_TAPS):
        acc = acc + sh[t * CONV_ROWS:(t + 1) * CONV_ROWS] * w_ref[k:k + 1, :]
    o_ref[pl.ds(_aligned(lo, CONV_ROWS), CONV_ROWS), :] = _silu(acc).astype(BF16)


def _conv_kernel(x_ref, w_ref, b_ref, o_ref):
    shift = _conv_shift_matrix()

    def body(t, carry):
        for u in range(CONV_UNROLL):
            _conv_step(x_ref, w_ref, b_ref, o_ref, shift, CONV_UNROLL * t + u)
        return carry

    lax.fori_loop(0, x_ref.shape[0] // (CONV_UNROLL * CONV_ROWS), body, 0)


def _ssd_conv(packed3, conv_w, conv_b):
    b, l, _ = packed3.shape
    assert l % (CONV_UNROLL * CONV_ROWS) == 0
    c0 = OFF_XBC // CONV_TN
    return pl.pallas_call(
        _conv_kernel,
        grid=(b, CONV_DIM // CONV_TN),
        in_specs=[
            pl.BlockSpec((None, l, CONV_TN), lambda i, j: (i, 0, c0 + j)),
            pl.BlockSpec((SSD_CONV, CONV_TN), lambda i, j: (0, j)),
            pl.BlockSpec((1, CONV_TN), lambda i, j: (0, j)),
        ],
        out_specs=pl.BlockSpec((None, l, CONV_TN), lambda i, j: (i, 0, j)),
        out_shape=jax.ShapeDtypeStruct((b, l, CONV_DIM), BF16),
        compiler_params=_cparams(2),
        name="ssd_conv",
    )(packed3, conv_w, conv_b)


GW = SSD_HPG * SSD_HEAD_DIM


def _split3(v):
    hi = v.astype(BF16)
    r1 = v - hi.astype(F32)
    mid = r1.astype(BF16)
    lo = (r1 - mid.astype(F32)).astype(BF16)
    return hi, mid, lo


ND = 2 * SSD_HPG
NDT = 2 * SSD_HEADS
PIECE_W = 6 * NDT
SPREAD_W = SSD_HPG * CHUNK + GW


def _group_major_perm():
    return [d * SSD_HEADS + g * SSD_HPG + r for g in range(SSD_GROUPS) for d in range(2) for r in range(SSD_HPG)]


def _piece_layout():
    import numpy as np
    m = np.zeros((PIECE_W, SSD_GROUPS * CHUNK), np.float32)
    for q in range(6):
        for g in range(SSD_GROUPS):
            for k in range(ND):
                m[q * NDT + ND * g + k, CHUNK * g + ND * q + k] = 1.0
    return jnp.asarray(m, BF16)


def _spread_layout():
    import numpy as np
    m = np.zeros((2, CHUNK, SPREAD_W), np.float32)
    for d in range(2):
        for q in range(6):
            for r in range(SSD_HPG):
                row = ND * q + SSD_HPG * d + r
                if q < 3:
                    m[d, row, CHUNK * r:CHUNK * (r + 1)] = 1.0
                else:
                    m[d, row, SSD_HPG * CHUNK + SSD_HEAD_DIM * r:SSD_HPG * CHUNK + SSD_HEAD_DIM * (r + 1)] = 1.0
    return jnp.asarray(m, BF16)


def _prep_kernel(dt_ref, alog_ref, lay_ref, pcs_ref, acsr_ref):
    neg_a = -jnp.exp(alog_ref[...]) * LOG2E
    ii = lax.broadcasted_iota(jnp.int32, (CHUNK, CHUNK), 0)
    jj = lax.broadcasted_iota(jnp.int32, (CHUNK, CHUNK), 1)
    tri_f = jnp.where(jj <= ii, 1.0, 0.0).astype(BF16)
    tri_b = jnp.where(jj >= ii, 1.0, 0.0).astype(BF16)
    col = lax.broadcasted_iota(jnp.int32, (1, 3 * NDT), 1)
    is_fwd = (col % ND) < SSD_HPG
    er = lax.broadcasted_iota(jnp.int32, (NDT, NDT), 0)
    ec = lax.broadcasted_iota(jnp.int32, (NDT, NDT), 1)
    eye = jnp.where(er == ec, 1.0, 0.0).astype(BF16)
    for c in range(PREP_CHUNKS):
        rows = slice(c * CHUNK, (c + 1) * CHUNK)
        dt = dt_ref[rows, :]
        a3 = jnp.concatenate(_split3(dt * neg_a), axis=-1)
        s3 = jnp.where(is_fwd, jnp.dot(tri_f, a3, preferred_element_type=F32),
                       jnp.dot(tri_b, a3, preferred_element_type=F32))
        acs = s3[:, 0:NDT] + s3[:, NDT:2 * NDT] + s3[:, 2 * NDT:3 * NDT]
        acs3 = _split3(acs)
        pieces = jnp.concatenate(acs3 + _split3(dt), axis=-1)
        pcs_ref[rows, :] = jnp.dot(pieces, lay_ref[...], preferred_element_type=F32).astype(BF16)
        tr = None
        for p in acs3:
            d = lax.dot_general(eye, p, (((1,), (1,)), ((), ())), preferred_element_type=F32)
            tr = d if tr is None else tr + d
        acsr_ref[c] = tr


PREP_CHUNKS = 8


def _ssd_prep(dt3, a_log):
    b, l, _ = dt3.shape
    nc = l // CHUNK
    assert nc % PREP_CHUNKS == 0
    rows = PREP_CHUNKS * CHUNK
    return pl.pallas_call(
        _prep_kernel,
        grid=(b, nc // PREP_CHUNKS),
        in_specs=[
            pl.BlockSpec((None, rows, NDT), lambda i, c: (i, c, 0)),
            pl.BlockSpec((1, NDT), lambda i, c: (0, 0)),
            pl.BlockSpec((PIECE_W, SSD_GROUPS * CHUNK), lambda i, c: (0, 0)),
        ],
        out_specs=[
            pl.BlockSpec((None, rows, SSD_GROUPS * CHUNK), lambda i, c: (i, c, 0)),
            pl.BlockSpec((None, PREP_CHUNKS, NDT, CHUNK), lambda i, c: (i, c, 0, 0)),
        ],
        out_shape=[
            jax.ShapeDtypeStruct((b, l, SSD_GROUPS * CHUNK), BF16),
            jax.ShapeDtypeStruct((b, nc, NDT, CHUNK), F32),
        ],
        compiler_params=_cparams(2),
        name="ssd_prep",
    )(dt3, a_log, _piece_layout())


def _ssd_kernel(x_ref, b_ref, c_ref, z_ref, pcs_ref, acsr_ref, lay_ref, dskip_ref, ng_ref, o_ref,
                yp_ref, st_ref, g_ref, xdt_ref, xw_ref, ea_ref, et_ref, *, nc):
    ii = lax.broadcasted_iota(jnp.int32, (CHUNK, CHUNK), 0)
    jj = lax.broadcasted_iota(jnp.int32, (CHUNK, CHUNK), 1)
    keep_f = jj <= ii
    keep_b = jj >= ii
    low_half = lax.broadcasted_iota(jnp.int32, (1, CHUNK), 1) < SSD_HEAD_DIM
    head_of_lane = lax.broadcasted_iota(jnp.int32, (1, GW), 1) // SSD_HEAD_DIM
    st_ref[...] = jnp.zeros_like(st_ref)

    def chunk_rows(c):
        return pl.ds(_aligned(c * CHUNK, CHUNK), CHUNK)

    def decay_stage(c, d, par):
        slot = 2 * par + d
        keep = keep_b if d else keep_f
        rows = chunk_rows(c)
        sp = jnp.dot(pcs_ref[rows, :], lay_ref[d], preferred_element_type=F32)
        acs_w = [sp[:, r * CHUNK:(r + 1) * CHUNK] for r in range(SSD_HPG)]
        dt_e = sp[:, SSD_HPG * CHUNK:]
        acs_e = jnp.concatenate([jnp.where(low_half, acs_w[0], acs_w[1]),
                                 jnp.where(low_half, acs_w[2], acs_w[3])], axis=-1)
        acs_r = acsr_ref[c]
        xdt = x_ref[rows, :].astype(F32) * dt_e
        scores = lax.dot_general(c_ref[rows, :], b_ref[rows, :], (((1,), (1,)), ((), ())),
                                 preferred_element_type=F32)
        for r in range(SSD_HPG):
            seg = acs_w[r] - acs_r[SSD_HPG * d + r:SSD_HPG * d + r + 1, :]
            g_ref[slot, :, r * CHUNK:(r + 1) * CHUNK] = (scores * jnp.exp2(jnp.where(keep, seg, NEG))).astype(BF16)
        total = acs_e[0:1, :] if d else acs_e[CHUNK - 1:CHUNK, :]
        xdt_ref[slot] = xdt.astype(BF16)
        xw_ref[slot] = (xdt * jnp.exp2(total - acs_e)).astype(BF16)
        ea_ref[slot] = jnp.exp2(acs_e)
        et_ref[slot] = jnp.exp2(total)

    def state_stage(c, d, par):
        slot = 2 * par + d
        rows = chunk_rows(c)
        xdt_b = xdt_ref[slot]
        zero = jnp.zeros_like(xdt_b)
        xbd = jnp.concatenate([jnp.where(head_of_lane == r, xdt_b, zero) for r in range(SSD_HPG)], axis=0)
        cm = c_ref[rows, :]
        st = st_ref[d]
        y = (jnp.dot(g_ref[slot], xbd, preferred_element_type=F32)
             + jnp.dot(cm, st.astype(BF16), preferred_element_type=F32) * ea_ref[slot])
        st_ref[d] = st * et_ref[slot] + lax.dot_general(
            b_ref[rows, :], xw_ref[slot], (((0,), (0,)), ((), ())), preferred_element_type=F32)
        return rows, y

    def first_visit(c, d, par):
        rows, y = state_stage(c, d, par)
        yp_ref[rows, :] = y

    def second_visit(c, d, par):
        rows, y = state_stage(c, d, par)
        xf = x_ref[rows, :].astype(F32)
        yt = (yp_ref[rows, :] + y + xf * dskip_ref[...]) * _silu(z_ref[rows, :].astype(F32))
        o_ref[rows, :] = _rms(yt, ng_ref[...]).astype(BF16)

    def step(t, par, visit):
        visit(t, 0, par)
        visit(nc - 1 - t, 1, par)
        decay_stage(t + 1, 0, 1 - par)
        decay_stage(nc - 2 - t, 1, 1 - par)

    def pair_of_steps(visit, t0):
        def body(k, carry):
            step(t0 + 2 * k, 0, visit)
            step(t0 + 2 * k + 1, 1, visit)
            return carry
        return body

    decay_stage(0, 0, 0)
    decay_stage(nc - 1, 1, 0)
    lax.fori_loop(0, nc // 4, pair_of_steps(first_visit, 0), 0)
    lax.fori_loop(0, nc // 4 - 1, pair_of_steps(second_visit, nc // 2), 0)
    step(nc - 2, 0, second_visit)
    second_visit(nc - 1, 0, 1)
    second_visit(0, 1, 1)


def _ssd(xbc3, packed3, pcs3, acsr4, d_skip_e, norm_g):
    b, l, _ = xbc3.shape
    nc = l // CHUNK
    assert nc % 4 == 0
    b0 = SSD_INNER // SSD_STATE
    seq = lambda w, j0: pl.BlockSpec((None, l, w), lambda i, g: (i, 0, j0 + g))
    return pl.pallas_call(
        functools.partial(_ssd_kernel, nc=nc),
        grid=(b, SSD_GROUPS),
        in_specs=[
            seq(GW, 0), seq(SSD_STATE, b0), seq(SSD_STATE, b0 + SSD_GROUPS), seq(GW, OFF_Z // GW),
            seq(CHUNK, 0),
            pl.BlockSpec((None, nc, ND, CHUNK), lambda i, g: (i, 0, g, 0)),
            pl.BlockSpec((2, CHUNK, SPREAD_W), lambda i, g: (0, 0, 0)),
            pl.BlockSpec((1, GW), lambda i, g: (0, g)),
            pl.BlockSpec((1, GW), lambda i, g: (0, g)),
        ],
        out_specs=seq(GW, 0),
        out_shape=jax.ShapeDtypeStruct((b, l, SSD_INNER), BF16),
        scratch_shapes=[
            pltpu.VMEM((l, GW), F32),
            pltpu.VMEM((2, SSD_STATE, GW), F32),
            pltpu.VMEM((4, CHUNK, SSD_HPG * CHUNK), BF16),
            pltpu.VMEM((4, CHUNK, GW), BF16),
            pltpu.VMEM((4, CHUNK, GW), BF16),
            pltpu.VMEM((4, CHUNK, GW), F32),
            pltpu.VMEM((4, 1, GW), F32),
        ],
        compiler_params=_cparams(2),
        name="ssd",
    )(xbc3, xbc3, xbc3, packed3, pcs3, acsr4, _spread_layout(), d_skip_e, norm_g)


MERGE_TM = 1024


def _merge_kernel(att_ref, y_ref, ga_ref, gs_ref, x_ref, ba_ref, bs_ref, wa_ref, ws_ref, wo_ref, o_ref):
    m1 = jnp.dot(att_ref[...], wa_ref[...], preferred_element_type=F32)
    m2 = jnp.dot(y_ref[...], ws_ref[...], preferred_element_type=F32)
    merged = (jax.nn.sigmoid(ga_ref[...].astype(F32) + ba_ref[...]) * m1
              + jax.nn.sigmoid(gs_ref[...].astype(F32) + bs_ref[...]) * m2)
    o_ref[...] = x_ref[...] + jnp.dot(merged.astype(BF16), wo_ref[...], preferred_element_type=F32)


def _merge(att2, y2, packed2, x2, b_gate, wa, ws, wo):
    m = x2.shape[0]
    tm = min(MERGE_TM, m)
    row = lambda w, j=0: pl.BlockSpec((tm, w), lambda i: (i, j))
    full = lambda r, c, j=0: pl.BlockSpec((r, c), lambda i: (0, j))
    return pl.pallas_call(
        _merge_kernel,
        grid=(m // tm,),
        in_specs=[
            row(ATT_W), row(SSD_INNER), row(D_MODEL, OFF_GA // D_MODEL), row(D_MODEL, OFF_GS // D_MODEL),
            row(D_MODEL), full(1, D_MODEL, 0), full(1, D_MODEL, 1),
            full(ATT_W, D_MODEL), full(SSD_INNER, D_MODEL), full(D_MODEL, D_MODEL),
        ],
        out_specs=row(D_MODEL),
        out_shape=jax.ShapeDtypeStruct((m, D_MODEL), F32),
        compiler_params=_cparams(1),
        name="merge",
    )(att2, y2, packed2, packed2, x2, b_gate, b_gate, wa, ws, wo)


FFN_TM = 1024
FFN_TF = 1408
FFN_HALO = 8


def _ffn_kernel(prev_ref, x_ref, next_ref, g_ref, wa_ref, wg_ref, cwa_ref, cwg_ref, cba_ref, cbg_ref,
                wd_ref, o_ref, h_ref, acc_ref):
    m = pl.program_id(1)
    f = pl.program_id(2)
    tm = x_ref.shape[0]

    @pl.when(f == 0)
    def _():
        prev = jnp.where(m > 0, prev_ref[...], 0.0)
        nxt = jnp.where(m < pl.num_programs(1) - 1, next_ref[...], 0.0)
        ext = jnp.concatenate([prev, x_ref[...], nxt], axis=0)
        h_ref[...] = _rms(ext, g_ref[...]).astype(BF16)
        acc_ref[...] = jnp.zeros_like(acc_ref)

    h = h_ref[...]

    def conv(w_ref, cw_ref, cb_ref):
        u = jnp.dot(h, w_ref[...], preferred_element_type=F32)
        out = jnp.broadcast_to(cb_ref[...], (tm, u.shape[1]))
        for k in range(FFN_CONV):
            out = out + _roll_rows(u, k - FFN_CONV // 2, FFN_HALO, tm) * cw_ref[k:k + 1, :]
        return out

    act = conv(wa_ref, cwa_ref, cba_ref) * _silu(conv(wg_ref, cwg_ref, cbg_ref))
    acc_ref[...] += jnp.dot(act.astype(BF16), wd_ref[...], preferred_element_type=F32)

    @pl.when(f == pl.num_programs(2) - 1)
    def _():
        o_ref[...] = x_ref[...] + acc_ref[...]


def _ffn(x3, g_ffn, w_up, conv_w, conv_b, w_down):
    b, l, _ = x3.shape
    tm = min(FFN_TM, l)
    nf = D_FF // FFN_TF
    hb = tm // FFN_HALO
    wcol = lambda r, j0: pl.BlockSpec((r, FFN_TF), lambda i, m, f: (0, j0 + f))
    return pl.pallas_call(
        _ffn_kernel,
        grid=(b, l // tm, nf),
        in_specs=[
            pl.BlockSpec((None, FFN_HALO, D_MODEL), lambda i, m, f: (i, jnp.maximum(m * hb - 1, 0), 0)),
            pl.BlockSpec((None, tm, D_MODEL), lambda i, m, f: (i, m, 0)),
            pl.BlockSpec((None, FFN_HALO, D_MODEL),
                         lambda i, m, f: (i, jnp.minimum((m + 1) * hb, l // FFN_HALO - 1), 0)),
            pl.BlockSpec((1, D_MODEL), lambda i, m, f: (0, 0)),
            wcol(D_MODEL, 0), wcol(D_MODEL, nf),
            wcol(FFN_CONV, 0), wcol(FFN_CONV, nf),
            wcol(1, 0), wcol(1, nf),
            pl.BlockSpec((FFN_TF, D_MODEL), lambda i, m, f: (f, 0)),
        ],
        out_specs=pl.BlockSpec((None, tm, D_MODEL), lambda i, m, f: (i, m, 0)),
        out_shape=jax.ShapeDtypeStruct((b, l, D_MODEL), F32),
        scratch_shapes=[pltpu.VMEM((tm + 2 * FFN_HALO, D_MODEL), BF16), pltpu.VMEM((tm, D_MODEL), F32)],
        compiler_params=_cparams(3),
        name="ffn",
    )(x3, x3, x3, g_ffn, w_up, w_up, conv_w, conv_w, conv_b, conv_b, w_down)


def _layer(x, bias_tab, p):
    b, l, _ = x.shape
    x2 = x.reshape(b * l, D_MODEL)
    packed2, dt2, kt2 = _proj(x2, p["g_mix"], p["w_main"], p["w_dt"], p["dt_bias"], p["w_kt"], p["k_norm_g"])
    packed3 = packed2.reshape(b, l, PROJ_W)
    att = _attention(packed3, kt2, p["q_norm_g"], bias_tab)
    xbc = _ssd_conv(packed3, p["ssd_conv_w"], p["ssd_conv_b"])
    pcs, acsr = _ssd_prep(dt2.reshape(b, l, NDT), p["a_log"])
    y = _ssd(xbc, packed3, pcs, acsr, p["d_skip_e"], p["ssd_norm_g"])
    x1 = _merge(att.reshape(b * l, ATT_W), y.reshape(b * l, SSD_INNER), packed2, x2,
                p["b_gate"], p["w_att_out"], p["w_ssd_out"], p["w_o"])
    return _ffn(x1.reshape(b, l, D_MODEL), p["g_ffn"], p["w_up"], p["ffn_conv_w"], p["ffn_conv_b"],
                p["w_down"])


def kernel(x_prompt, x_sample, g_mix, w_in, b_gate, q_norm_g, k_norm_g, rel_bias, ssd_conv_w, ssd_conv_b,
           dt_bias_f, dt_bias_b, A_log_f, A_log_b, D_skip, ssd_norm_g, w_att_out, w_ssd_out, w_o, g_ffn,
           w_up, ffn_conv_w, ffn_conv_b, w_down):
    depth = g_mix.shape[0]
    xs = [x_prompt, x_sample]
    perm = jnp.asarray(_group_major_perm(), jnp.int32)
    for layer in range(depth):
        w = w_in[layer]
        p = {
            "g_mix": g_mix[layer][None, :],
            "w_main": jnp.concatenate([w[:, :ATT_W], w[:, 2 * ATT_W:DT_COL], w[:, DT_COL + 2 * SSD_HEADS:]],
                                      axis=1).astype(BF16),
            "w_kt": w[:, ATT_W:2 * ATT_W].T.astype(BF16),
            "w_dt": w[:, DT_COL:DT_COL + 2 * SSD_HEADS][:, perm].astype(BF16),
            "dt_bias": jnp.concatenate([dt_bias_f[layer], dt_bias_b[layer]])[perm][None, :],
            "q_norm_g": q_norm_g[layer],
            "k_norm_g": k_norm_g[layer],
            "ssd_conv_w": ssd_conv_w[layer],
            "ssd_conv_b": ssd_conv_b[layer][None, :],
            "a_log": jnp.concatenate([A_log_f[layer], A_log_b[layer]])[perm][None, :],
            "d_skip_e": jnp.repeat(D_skip[layer], SSD_HEAD_DIM)[None, :],
            "ssd_norm_g": ssd_norm_g[layer][None, :],
            "b_gate": b_gate[layer][None, :],
            "w_att_out": w_att_out[layer].astype(BF16),
            "w_ssd_out": w_ssd_out[layer].astype(BF16),
            "w_o": w_o[layer].astype(BF16),
            "g_ffn": g_ffn[layer][None, :],
            "w_up": w_up[layer].astype(BF16),
            "ffn_conv_w": ffn_conv_w[layer],
            "ffn_conv_b": ffn_conv_b[layer][None, :],
            "w_down": w_down[layer].astype(BF16),
        }
        bias_tab = _bias_table(rel_bias[layer])
        xs = [_layer(x, bias_tab, p) for x in xs]
    return tuple(xs)
```

```python
import functools

import jax
import jax.numpy as jnp
from jax import lax
from jax.experimental import pallas as pl
from jax.experimental.pallas import tpu as pltpu

F32 = jnp.float32
BF16 = jnp.bfloat16
HIGHEST = lax.Precision.HIGHEST

D_MODEL = 1024
GRID_W = 64
WIN_R = 8
WIN_C = 16
ATT_HEADS = 16
HEAD_DIM = 64
ATT_W = ATT_HEADS * HEAD_DIM
SSD_INNER = 2 * D_MODEL
SSD_HEAD_DIM = 64
SSD_HEADS = SSD_INNER // SSD_HEAD_DIM
SSD_GROUPS = 8
SSD_HPG = SSD_HEADS // SSD_GROUPS
SSD_STATE = 128
SSD_CONV = 5
CONV_DIM = SSD_INNER + 2 * SSD_GROUPS * SSD_STATE
CHUNK = 128
D_FF = 2816
FFN_CONV = 3
EPS = 1e-6

OFF_Q = 0
OFF_V = OFF_Q + ATT_W
OFF_Z = OFF_V + ATT_W
OFF_XBC = OFF_Z + SSD_INNER
OFF_GA = OFF_XBC + CONV_DIM
OFF_GS = OFF_GA + D_MODEL
PROJ_W = OFF_GS + D_MODEL
DT_COL = 3 * ATT_W + SSD_INNER + CONV_DIM

NEG = -1e30
LOG2E = 1.4426950408889634
VMEM_LIMIT = 56 * 1024 * 1024


def _cparams(n_axes):
    return pltpu.CompilerParams(
        dimension_semantics=("arbitrary",) * n_axes, vmem_limit_bytes=VMEM_LIMIT)


def _rms(x, g):
    return x * lax.rsqrt(jnp.mean(x * x, axis=-1, keepdims=True) + EPS) * g


def _silu(x):
    return x * jax.nn.sigmoid(x)


PROJ_TM = 1024
PROJ_TN = 2048


PROJ_NT = PROJ_W // PROJ_TN
KT_COLS = 256


def _proj_kernel(x_ref, g_ref, w_ref, wdt_ref, dtb_ref, wkt_ref, gk_ref, o_ref, dt_ref, kt_ref, h_ref):
    j = pl.program_id(1)

    @pl.when(j == 0)
    def _():
        h = _rms(x_ref[...], g_ref[...]).astype(BF16)
        h_ref[...] = h
        raw = jnp.dot(h, wdt_ref[...], preferred_element_type=F32)
        dt_ref[...] = jax.nn.softplus(raw + dtb_ref[...])

    @pl.when(j < PROJ_NT)
    def _():
        o_ref[...] = jnp.dot(h_ref[...], w_ref[...], preferred_element_type=F32).astype(BF16)

    @pl.when(j == PROJ_NT)
    def _():
        gk = gk_ref[...].reshape(1, HEAD_DIM, 1)
        for c in range(h_ref.shape[0] // KT_COLS):
            cols = slice(c * KT_COLS, (c + 1) * KT_COLS)
            kt = lax.dot_general(wkt_ref[...], h_ref[cols, :], (((1,), (1,)), ((), ())),
                                 preferred_element_type=F32)
            kt = kt.reshape(ATT_HEADS, HEAD_DIM, KT_COLS)
            ms = jnp.mean(kt * kt, axis=1, keepdims=True)
            kt_ref[:, cols] = (kt * lax.rsqrt(ms + EPS) * gk).reshape(ATT_W, KT_COLS).astype(BF16)


def _proj(x2, g_mix, w_main, w_dt, dt_bias, w_kt, k_norm_g):
    m = x2.shape[0]
    tm = min(PROJ_TM, m)
    last = PROJ_NT - 1
    return pl.pallas_call(
        _proj_kernel,
        grid=(m // tm, PROJ_NT + 1),
        in_specs=[
            pl.BlockSpec((tm, D_MODEL), lambda i, j: (i, 0)),
            pl.BlockSpec((1, D_MODEL), lambda i, j: (0, 0)),
            pl.BlockSpec((D_MODEL, PROJ_TN), lambda i, j: (0, jnp.minimum(j, last))),
            pl.BlockSpec((D_MODEL, 2 * SSD_HEADS), lambda i, j: (0, 0)),
            pl.BlockSpec((1, 2 * SSD_HEADS), lambda i, j: (0, 0)),
            pl.BlockSpec((ATT_W, D_MODEL), lambda i, j: (0, 0)),
            pl.BlockSpec((HEAD_DIM, 1), lambda i, j: (0, 0)),
        ],
        out_specs=[
            pl.BlockSpec((tm, PROJ_TN), lambda i, j: (i, jnp.minimum(j, last))),
            pl.BlockSpec((tm, 2 * SSD_HEADS), lambda i, j: (i, 0)),
            pl.BlockSpec((ATT_W, tm), lambda i, j: (0, i)),
        ],
        out_shape=[
            jax.ShapeDtypeStruct((m, PROJ_W), BF16),
            jax.ShapeDtypeStruct((m, 2 * SSD_HEADS), F32),
            jax.ShapeDtypeStruct((ATT_W, m), BF16),
        ],
        scratch_shapes=[pltpu.VMEM((tm, D_MODEL), BF16)],
        compiler_params=_cparams(2),
        name="proj",
    )(x2, g_mix, w_main, w_dt, dt_bias, w_kt, k_norm_g.reshape(HEAD_DIM, 1))


N_ROFF = 2 * WIN_R - 1
N_COFF = 2 * WIN_C - 1
KEYS = WIN_R * GRID_W


def _bias_kernel(rb_ref, o_ref):
    h = pl.program_id(0)
    c = lax.broadcasted_iota(jnp.int32, (GRID_W, GRID_W), 0)
    cp = lax.broadcasted_iota(jnp.int32, (GRID_W, GRID_W), 1)
    cs = jnp.clip(c - WIN_C // 2, 0, GRID_W - WIN_C)
    inside = jnp.where(cp >= cs, jnp.where(cp < cs + WIN_C, 1, 0), 0)
    idx = jnp.clip(cp - c + WIN_C - 1, 0, N_COFF - 1)
    tiles = []
    for ro in range(N_ROFF):
        base = (h * N_ROFF + ro) * N_COFF

        def body(k, acc, base=base):
            return jnp.where(idx == k, rb_ref[base + k], acc)

        t = lax.fori_loop(0, N_COFF, body, jnp.zeros((GRID_W, GRID_W), F32))
        tiles.append(jnp.where(inside == 1, t, NEG))
    for d in range(WIN_R):
        for i in range(WIN_R):
            o_ref[0, d, :, i * GRID_W:(i + 1) * GRID_W] = tiles[i - d + WIN_R - 1]


def _bias_table(rel_bias):
    return pl.pallas_call(
        _bias_kernel,
        grid=(ATT_HEADS,),
        in_specs=[pl.BlockSpec(memory_space=pltpu.SMEM)],
        out_specs=pl.BlockSpec((1, WIN_R, GRID_W, KEYS), lambda h: (h // 2, 0, h % 2, 0)),
        out_shape=jax.ShapeDtypeStruct((ATT_HEADS // 2, WIN_R, 2 * GRID_W, KEYS), F32),
        compiler_params=_cparams(1),
        name="bias_table",
    )(rel_bias.reshape(-1))


NORM_ROWS = 512


ATT_ROWS = 8
SHIFT_COLS = 256


def _aligned(x, m):
    return x if isinstance(x, int) else pl.multiple_of(x, m)


def _attn_kernel(q_ref, kt_in_ref, v_ref, gq_ref, bias_ref, o_ref, qn_ref, kt_ref, vx_ref, s_ref, p_ref, *, rows):
    l = rows * GRID_W
    pair = 2 * HEAD_DIM
    lane = lax.broadcasted_iota(jnp.int32, (2 * pair, pair), 0) % pair // HEAD_DIM
    lane_t = lax.broadcasted_iota(jnp.int32, (2 * pair, pair), 1) // HEAD_DIM
    seg = jnp.where(lane == lane_t, 1.0 / HEAD_DIM, 0.0).astype(BF16)

    def norm_body(i, carry):
        sl = pl.ds(pl.multiple_of(i * NORM_ROWS, NORM_ROWS), NORM_ROWS)
        x = q_ref[sl, :].astype(F32)
        sq = x * x
        hi = sq.astype(BF16)
        lo = (sq - hi.astype(F32)).astype(BF16)
        ms = jnp.dot(jnp.concatenate([hi, lo], axis=-1), seg, preferred_element_type=F32)
        qn_ref[sl, :] = (x * lax.rsqrt(ms + EPS) * gq_ref[...] * (HEAD_DIM ** -0.5)).astype(BF16)
        vx_ref[sl, 0:pair] = v_ref[sl, :]
        vx_ref[sl, pair:2 * pair] = jnp.ones((NORM_ROWS, pair), BF16)
        return carry

    lax.fori_loop(0, l // NORM_ROWS, norm_body, 0)

    ext = SHIFT_COLS + 2 * GRID_W

    def shifted(c0):
        words = pltpu.bitcast(kt_in_ref[:, pl.ds(c0, ext)], jnp.uint32)
        return pltpu.bitcast(pltpu.roll(words, ext - GRID_W, 1), BF16)

    def shift_body(i, carry):
        c0 = pl.multiple_of(i * SHIFT_COLS, SHIFT_COLS)
        kt_ref[0, :, pl.ds(c0, SHIFT_COLS)] = kt_in_ref[:, pl.ds(c0, SHIFT_COLS)]
        kt_ref[1, :, pl.ds(c0, SHIFT_COLS)] = shifted(c0)[:, :SHIFT_COLS]
        return carry

    n_full = (l - ext) // SHIFT_COLS + 1
    lax.fori_loop(0, n_full, shift_body, 0)
    for c0 in range(n_full * SHIFT_COLS, l, SHIFT_COLS):
        kt_ref[0, :, c0:c0 + SHIFT_COLS] = kt_in_ref[:, c0:c0 + SHIFT_COLS]
    kt_ref[1, :, l - SHIFT_COLS:l - GRID_W] = shifted(l - ext)[:, 2 * GRID_W:ext - GRID_W]

    top = lax.broadcasted_iota(jnp.int32, (1, pair), 1) < HEAD_DIM

    def window(r):
        if isinstance(r, int):
            rs = min(max(r - WIN_R // 2, 0), rows - WIN_R)
            return rs, r - rs, rs % 2
        rs = jnp.clip(r - WIN_R // 2, 0, rows - WIN_R)
        return rs, r - rs, lax.rem(rs, 2)

    def scores_stage(r, slot):
        rs, d, odd = window(r)
        q2 = qn_ref[pl.ds(_aligned(r * GRID_W, GRID_W), GRID_W), :]
        zero = jnp.zeros_like(q2)
        qq = jnp.concatenate([jnp.where(top, q2, zero), jnp.where(top, zero, q2)], axis=0)
        kcol = pl.ds(_aligned((rs - odd) * GRID_W, 2 * GRID_W), KEYS)
        s_ref[slot] = jnp.dot(qq, kt_ref[odd, :, kcol], preferred_element_type=F32) + bias_ref[d]

    def softmax_stage(slot):
        s = s_ref[slot]
        p_ref[slot] = jnp.exp(s - jnp.max(s, axis=-1, keepdims=True)).astype(BF16)

    def values_stage(r, slot):
        rs, _, _ = window(r)
        vrow = pl.ds(_aligned(rs * GRID_W, GRID_W), KEYS)
        ov = jnp.dot(p_ref[slot], vx_ref[vrow, :], preferred_element_type=F32)
        num = jnp.where(top, ov[:GRID_W, :pair], ov[GRID_W:, :pair])
        den = jnp.where(top, ov[:GRID_W, pair:], ov[GRID_W:, pair:])
        o_ref[pl.ds(_aligned(r * GRID_W, GRID_W), GRID_W), :] = (num / den).astype(BF16)

    n_groups = rows // ATT_ROWS

    def stages(t, par):
        static = isinstance(t, int)
        mine, other = par * ATT_ROWS, (1 - par) * ATT_ROWS
        for u in range(ATT_ROWS):
            if not static or 0 <= t - 2 < n_groups:
                values_stage((t - 2) * ATT_ROWS + u, mine + u)
        for u in range(ATT_ROWS):
            if not static or 0 <= t - 1 < n_groups:
                softmax_stage(other + u)
        for u in range(ATT_ROWS):
            if not static or 0 <= t < n_groups:
                scores_stage(t * ATT_ROWS + u, mine + u)

    stages(0, 0)
    stages(1, 1)

    def body(k, carry):
        stages(2 * k, 0)
        stages(2 * k + 1, 1)
        return carry

    lax.fori_loop(1, n_groups // 2, body, 0)
    stages(n_groups, 0)
    stages(n_groups + 1, 1)


def _attention(packed3, kt2, q_norm_g, bias_tab):
    b, l, _ = packed3.shape
    rows = l // GRID_W
    assert rows % (2 * ATT_ROWS) == 0 and l % SHIFT_COLS == 0
    pair = 2 * HEAD_DIM
    gq = jnp.tile(q_norm_g.reshape(1, HEAD_DIM), (1, 2))
    col = lambda off: (lambda i, j: (i, 0, off // pair + j))
    return pl.pallas_call(
        functools.partial(_attn_kernel, rows=rows),
        grid=(b, ATT_HEADS // 2),
        in_specs=[
            pl.BlockSpec((None, l, pair), col(OFF_Q)),
            pl.BlockSpec((pair, l), lambda i, j: (j, i)),
            pl.BlockSpec((None, l, pair), col(OFF_V)),
            pl.BlockSpec((1, pair), lambda i, j: (0, 0)),
            pl.BlockSpec((None, WIN_R, 2 * GRID_W, KEYS), lambda i, j: (j, 0, 0, 0)),
        ],
        out_specs=pl.BlockSpec((None, l, pair), lambda i, j: (i, 0, j)),
        out_shape=jax.ShapeDtypeStruct((b, l, ATT_W), BF16),
        scratch_shapes=[pltpu.VMEM((l, pair), BF16), pltpu.VMEM((2, pair, l), BF16),
                        pltpu.VMEM((l, 2 * pair), BF16), pltpu.VMEM((2 * ATT_ROWS, 2 * GRID_W, KEYS), F32),
                        pltpu.VMEM((2 * ATT_ROWS, 2 * GRID_W, KEYS), BF16)],
        compiler_params=_cparams(2),
        name="attention",
    )(packed3, kt2, packed3, gq, bias_tab)


CONV_TN = 512
CONV_ROWS = 128
CONV_UNROLL = 8
HALO = 16


def _roll_rows(x, offset, lo, n):
    if offset == 0:
        return x[lo:lo + n]
    return pltpu.roll(x, (-offset) % x.shape[0], 0)[lo:lo + n]


CONV_TAPS = [k for k in range(SSD_CONV) if k != SSD_CONV // 2]


def _conv_shift_matrix():
    half = SSD_CONV // 2
    ext_rows = CONV_ROWS + 2 * HALO
    ri = lax.broadcasted_iota(jnp.int32, (len(CONV_TAPS) * CONV_ROWS, ext_rows), 0)
    ci = lax.broadcasted_iota(jnp.int32, (len(CONV_TAPS) * CONV_ROWS, ext_rows), 1)
    src = ri % CONV_ROWS + HALO - half + ri // CONV_ROWS
    src = jnp.where(ri // CONV_ROWS >= half, src + 1, src)
    return jnp.where(ci == src, 1.0, 0.0).astype(BF16)


def _conv_step(x_ref, w_ref, b_ref, o_ref, shift, c):
    l = x_ref.shape[0]
    half = SSD_CONV // 2
    lo = c * CONV_ROWS
    if isinstance(c, int):
        lo_before, lo_after = max(lo - HALO, 0), min(lo + CONV_ROWS, l - HALO)
    else:
        lo_before, lo_after = jnp.maximum(lo - HALO, 0), jnp.minimum(lo + CONV_ROWS, l - HALO)
    main = x_ref[pl.ds(_aligned(lo, CONV_ROWS), CONV_ROWS), :]
    before = x_ref[pl.ds(_aligned(lo_before, HALO), HALO), :]
    after = x_ref[pl.ds(_aligned(lo_after, HALO), HALO), :]
    zero = jnp.zeros_like(before)
    ext = jnp.concatenate([jnp.where(c > 0, before, zero), main,
                           jnp.where(c < l // CONV_ROWS - 1, after, zero)], axis=0)
    sh = jnp.dot(shift, ext, preferred_element_type=F32)
    acc = b_ref[...] + main.astype(F32) * w_ref[half:half + 1, :]
    for t, k in enumerate(CONV_TAPS):
        acc = acc + sh[t * CONV_ROWS:(t + 1) * CONV_ROWS] * w_ref[k:k + 1, :]
    o_ref[pl.ds(_aligned(lo, CONV_ROWS), CONV_ROWS), :] = _silu(acc).astype(BF16)


def _conv_kernel(x_ref, w_ref, b_ref, o_ref):
    shift = _conv_shift_matrix()

    def body(t, carry):
        for u in range(CONV_UNROLL):
            _conv_step(x_ref, w_ref, b_ref, o_ref, shift, CONV_UNROLL * t + u)
        return carry

    lax.fori_loop(0, x_ref.shape[0] // (CONV_UNROLL * CONV_ROWS), body, 0)


def _ssd_conv(packed3, conv_w, conv_b):
    b, l, _ = packed3.shape
    assert l % (CONV_UNROLL * CONV_ROWS) == 0
    c0 = OFF_XBC // CONV_TN
    return pl.pallas_call(
        _conv_kernel,
        grid=(b, CONV_DIM // CONV_TN),
        in_specs=[
            pl.BlockSpec((None, l, CONV_TN), lambda i, j: (i, 0, c0 + j)),
            pl.BlockSpec((SSD_CONV, CONV_TN), lambda i, j: (0, j)),
            pl.BlockSpec((1, CONV_TN), lambda i, j: (0, j)),
        ],
        out_specs=pl.BlockSpec((None, l, CONV_TN), lambda i, j: (i, 0, j)),
        out_shape=jax.ShapeDtypeStruct((b, l, CONV_DIM), BF16),
        compiler_params=_cparams(2),
        name="ssd_conv",
    )(packed3, conv_w, conv_b)


GW = SSD_HPG * SSD_HEAD_DIM


def _split3(v):
    hi = v.astype(BF16)
    r1 = v - hi.astype(F32)
    mid = r1.astype(BF16)
    lo = (r1 - mid.astype(F32)).astype(BF16)
    return hi, mid, lo


ND = 2 * SSD_HPG
NDT = 2 * SSD_HEADS
PIECE_W = 6 * NDT
SPREAD_W = SSD_HPG * CHUNK + GW


def _group_major_perm():
    return [d * SSD_HEADS + g * SSD_HPG + r for g in range(SSD_GROUPS) for d in range(2) for r in range(SSD_HPG)]


def _piece_layout():
    import numpy as np
    m = np.zeros((PIECE_W, SSD_GROUPS * CHUNK), np.float32)
    for q in range(6):
        for g in range(SSD_GROUPS):
            for k in range(ND):
                m[q * NDT + ND * g + k, CHUNK * g + ND * q + k] = 1.0
    return jnp.asarray(m, BF16)


def _spread_layout():
    import numpy as np
    m = np.zeros((2, CHUNK, SPREAD_W), np.float32)
    for d in range(2):
        for q in range(6):
            for r in range(SSD_HPG):
                row = ND * q + SSD_HPG * d + r
                if q < 3:
                    m[d, row, CHUNK * r:CHUNK * (r + 1)] = 1.0
                else:
                    m[d, row, SSD_HPG * CHUNK + SSD_HEAD_DIM * r:SSD_HPG * CHUNK + SSD_HEAD_DIM * (r + 1)] = 1.0
    return jnp.asarray(m, BF16)


def _prep_kernel(dt_ref, alog_ref, lay_ref, pcs_ref, acsr_ref):
    neg_a = -jnp.exp(alog_ref[...]) * LOG2E
    ii = lax.broadcasted_iota(jnp.int32, (CHUNK, CHUNK), 0)
    jj = lax.broadcasted_iota(jnp.int32, (CHUNK, CHUNK), 1)
    tri_f = jnp.where(jj <= ii, 1.0, 0.0).astype(BF16)
    tri_b = jnp.where(jj >= ii, 1.0, 0.0).astype(BF16)
    col = lax.broadcasted_iota(jnp.int32, (1, 3 * NDT), 1)
    is_fwd = (col % ND) < SSD_HPG
    er = lax.broadcasted_iota(jnp.int32, (NDT, NDT), 0)
    ec = lax.broadcasted_iota(jnp.int32, (NDT, NDT), 1)
    eye = jnp.where(er == ec, 1.0, 0.0).astype(BF16)
    for c in range(PREP_CHUNKS):
        rows = slice(c * CHUNK, (c + 1) * CHUNK)
        dt = dt_ref[rows, :]
        a3 = jnp.concatenate(_split3(dt * neg_a), axis=-1)
        s3 = jnp.where(is_fwd, jnp.dot(tri_f, a3, preferred_element_type=F32),
                       jnp.dot(tri_b, a3, preferred_element_type=F32))
        acs = s3[:, 0:NDT] + s3[:, NDT:2 * NDT] + s3[:, 2 * NDT:3 * NDT]
        acs3 = _split3(acs)
        pieces = jnp.concatenate(acs3 + _split3(dt), axis=-1)
        pcs_ref[rows, :] = jnp.dot(pieces, lay_ref[...], preferred_element_type=F32).astype(BF16)
        tr = None
        for p in acs3:
            d = lax.dot_general(eye, p, (((1,), (1,)), ((), ())), preferred_element_type=F32)
            tr = d if tr is None else tr + d
        acsr_ref[c] = tr


PREP_CHUNKS = 8


def _ssd_prep(dt3, a_log):
    b, l, _ = dt3.shape
    nc = l // CHUNK
    assert nc % PREP_CHUNKS == 0
    rows = PREP_CHUNKS * CHUNK
    return pl.pallas_call(
        _prep_kernel,
        grid=(b, nc // PREP_CHUNKS),
        in_specs=[
            pl.BlockSpec((None, rows, NDT), lambda i, c: (i, c, 0)),
            pl.BlockSpec((1, NDT), lambda i, c: (0, 0)),
            pl.BlockSpec((PIECE_W, SSD_GROUPS * CHUNK), lambda i, c: (0, 0)),
        ],
        out_specs=[
            pl.BlockSpec((None, rows, SSD_GROUPS * CHUNK), lambda i, c: (i, c, 0)),
            pl.BlockSpec((None, PREP_CHUNKS, NDT, CHUNK), lambda i, c: (i, c, 0, 0)),
        ],
        out_shape=[
            jax.ShapeDtypeStruct((b, l, SSD_GROUPS * CHUNK), BF16),
            jax.ShapeDtypeStruct((b, nc, NDT, CHUNK), F32),
        ],
        compiler_params=_cparams(2),
        name="ssd_prep",
    )(dt3, a_log, _piece_layout())


def _ssd_kernel(x_ref, b_ref, c_ref, z_ref, pcs_ref, acsr_ref, lay_ref, dskip_ref, ng_ref, o_ref,
                yp_ref, st_ref, g_ref, xdt_ref, xw_ref, ea_ref, et_ref, *, nc):
    ii = lax.broadcasted_iota(jnp.int32, (CHUNK, CHUNK), 0)
    jj = lax.broadcasted_iota(jnp.int32, (CHUNK, CHUNK), 1)
    keep_f = jj <= ii
    keep_b = jj >= ii
    low_half = lax.broadcasted_iota(jnp.int32, (1, CHUNK), 1) < SSD_HEAD_DIM
    head_of_lane = lax.broadcasted_iota(jnp.int32, (1, GW), 1) // SSD_HEAD_DIM
    st_ref[...] = jnp.zeros_like(st_ref)

    def chunk_rows(c):
        return pl.ds(_aligned(c * CHUNK, CHUNK), CHUNK)

    def decay_stage(c, d, par):
        slot = 2 * par + d
        keep = keep_b if d else keep_f
        rows = chunk_rows(c)
        sp = jnp.dot(pcs_ref[rows, :], lay_ref[d], preferred_element_type=F32)
        acs_w = [sp[:, r * CHUNK:(r + 1) * CHUNK] for r in range(SSD_HPG)]
        dt_e = sp[:, SSD_HPG * CHUNK:]
        acs_e = jnp.concatenate([jnp.where(low_half, acs_w[0], acs_w[1]),
                                 jnp.where(low_half, acs_w[2], acs_w[3])], axis=-1)
        acs_r = acsr_ref[c]
        xdt = x_ref[rows, :].astype(F32) * dt_e
        scores = lax.dot_general(c_ref[rows, :], b_ref[rows, :], (((1,), (1,)), ((), ())),
                                 preferred_element_type=F32)
        for r in range(SSD_HPG):
            seg = acs_w[r] - acs_r[SSD_HPG * d + r:SSD_HPG * d + r + 1, :]
            g_ref[slot, :, r * CHUNK:(r + 1) * CHUNK] = (scores * jnp.exp2(jnp.where(keep, seg, NEG))).astype(BF16)
        total = acs_e[0:1, :] if d else acs_e[CHUNK - 1:CHUNK, :]
        xdt_ref[slot] = xdt.astype(BF16)
        xw_ref[slot] = (xdt * jnp.exp2(total - acs_e)).astype(BF16)
        ea_ref[slot] = jnp.exp2(acs_e)
        et_ref[slot] = jnp.exp2(total)

    def state_stage(c, d, par):
        slot = 2 * par + d
        rows = chunk_rows(c)
        xdt_b = xdt_ref[slot]
        zero = jnp.zeros_like(xdt_b)
        xbd = jnp.concatenate([jnp.where(head_of_lane == r, xdt_b, zero) for r in range(SSD_HPG)], axis=0)
        cm = c_ref[rows, :]
        st = st_ref[d]
        y = (jnp.dot(g_ref[slot], xbd, preferred_element_type=F32)
             + jnp.dot(cm, st.astype(BF16), preferred_element_type=F32) * ea_ref[slot])
        st_ref[d] = st * et_ref[slot] + lax.dot_general(
            b_ref[rows, :], xw_ref[slot], (((0,), (0,)), ((), ())), preferred_element_type=F32)
        return rows, y

    def first_visit(c, d, par):
        rows, y = state_stage(c, d, par)
        yp_ref[rows, :] = y

    def second_visit(c, d, par):
        rows, y = state_stage(c, d, par)
        xf = x_ref[rows, :].astype(F32)
        yt = (yp_ref[rows, :] + y + xf * dskip_ref[...]) * _silu(z_ref[rows, :].astype(F32))
        o_ref[rows, :] = _rms(yt, ng_ref[...]).astype(BF16)

    def step(t, par, visit):
        visit(t, 0, par)
        visit(nc - 1 - t, 1, par)
        decay_stage(t + 1, 0, 1 - par)
        decay_stage(nc - 2 - t, 1, 1 - par)

    def pair_of_steps(visit, t0):
        def body(k, carry):
            step(t0 + 2 * k, 0, visit)
            step(t0 + 2 * k + 1, 1, visit)
            return carry
        return body

    decay_stage(0, 0, 0)
    decay_stage(nc - 1, 1, 0)
    lax.fori_loop(0, nc // 4, pair_of_steps(first_visit, 0), 0)
    lax.fori_loop(0, nc // 4 - 1, pair_of_steps(second_visit, nc // 2), 0)
    step(nc - 2, 0, second_visit)
    second_visit(nc - 1, 0, 1)
    second_visit(0, 1, 1)


def _ssd(xbc3, packed3, pcs3, acsr4, d_skip_e, norm_g):
    b, l, _ = xbc3.shape
    nc = l // CHUNK
    assert nc % 4 == 0
    b0 = SSD_INNER // SSD_STATE
    seq = lambda w, j0: pl.BlockSpec((None, l, w), lambda i, g: (i, 0, j0 + g))
    return pl.pallas_call(
        functools.partial(_ssd_kernel, nc=nc),
        grid=(b, SSD_GROUPS),
        in_specs=[
            seq(GW, 0), seq(SSD_STATE, b0), seq(SSD_STATE, b0 + SSD_GROUPS), seq(GW, OFF_Z // GW),
            seq(CHUNK, 0),
            pl.BlockSpec((None, nc, ND, CHUNK), lambda i, g: (i, 0, g, 0)),
            pl.BlockSpec((2, CHUNK, SPREAD_W), lambda i, g: (0, 0, 0)),
            pl.BlockSpec((1, GW), lambda i, g: (0, g)),
            pl.BlockSpec((1, GW), lambda i, g: (0, g)),
        ],
        out_specs=seq(GW, 0),
        out_shape=jax.ShapeDtypeStruct((b, l, SSD_INNER), BF16),
        scratch_shapes=[
            pltpu.VMEM((l, GW), F32),
            pltpu.VMEM((2, SSD_STATE, GW), F32),
            pltpu.VMEM((4, CHUNK, SSD_HPG * CHUNK), BF16),
            pltpu.VMEM((4, CHUNK, GW), BF16),
            pltpu.VMEM((4, CHUNK, GW), BF16),
            pltpu.VMEM((4, CHUNK, GW), F32),
            pltpu.VMEM((4, 1, GW), F32),
        ],
        compiler_params=_cparams(2),
        name="ssd",
    )(xbc3, xbc3, xbc3, packed3, pcs3, acsr4, _spread_layout(), d_skip_e, norm_g)


MERGE_TM = 1024


def _merge_kernel(att_ref, y_ref, ga_ref, gs_ref, x_ref, ba_ref, bs_ref, wa_ref, ws_ref, wo_ref, o_ref):
    m1 = jnp.dot(att_ref[...], wa_ref[...], preferred_element_type=F32)
    m2 = jnp.dot(y_ref[...], ws_ref[...], preferred_element_type=F32)
    merged = (jax.nn.sigmoid(ga_ref[...].astype(F32) + ba_ref[...]) * m1
              + jax.nn.sigmoid(gs_ref[...].astype(F32) + bs_ref[...]) * m2)
    o_ref[...] = x_ref[...] + jnp.dot(merged.astype(BF16), wo_ref[...], preferred_element_type=F32)


def _merge(att2, y2, packed2, x2, b_gate, wa, ws, wo):
    m = x2.shape[0]
    tm = min(MERGE_TM, m)
    row = lambda w, j=0: pl.BlockSpec((tm, w), lambda i: (i, j))
    full = lambda r, c, j=0: pl.BlockSpec((r, c), lambda i: (0, j))
    return pl.pallas_call(
        _merge_kernel,
        grid=(m // tm,),
        in_specs=[
            row(ATT_W), row(SSD_INNER), row(D_MODEL, OFF_GA // D_MODEL), row(D_MODEL, OFF_GS // D_MODEL),
            row(D_MODEL), full(1, D_MODEL, 0), full(1, D_MODEL, 1),
            full(ATT_W, D_MODEL), full(SSD_INNER, D_MODEL), full(D_MODEL, D_MODEL),
        ],
        out_specs=row(D_MODEL),
        out_shape=jax.ShapeDtypeStruct((m, D_MODEL), F32),
        compiler_params=_cparams(1),
        name="merge",
    )(att2, y2, packed2, packed2, x2, b_gate, b_gate, wa, ws, wo)


FFN_TM = 1024
FFN_TF = 1408
FFN_HALO = 8


def _ffn_kernel(prev_ref, x_ref, next_ref, g_ref, wa_ref, wg_ref, cwa_ref, cwg_ref, cba_ref, cbg_ref,
                wd_ref, o_ref, h_ref, acc_ref):
    m = pl.program_id(1)
    f = pl.program_id(2)
    tm = x_ref.shape[0]

    @pl.when(f == 0)
    def _():
        prev = jnp.where(m > 0, prev_ref[...], 0.0)
        nxt = jnp.where(m < pl.num_programs(1) - 1, next_ref[...], 0.0)
        ext = jnp.concatenate([prev, x_ref[...], nxt], axis=0)
        h_ref[...] = _rms(ext, g_ref[...]).astype(BF16)
        acc_ref[...] = jnp.zeros_like(acc_ref)

    h = h_ref[...]

    def conv(w_ref, cw_ref, cb_ref):
        u = jnp.dot(h, w_ref[...], preferred_element_type=F32)
        out = jnp.broadcast_to(cb_ref[...], (tm, u.shape[1]))
        for k in range(FFN_CONV):
            out = out + _roll_rows(u, k - FFN_CONV // 2, FFN_HALO, tm) * cw_ref[k:k + 1, :]
        return out

    act = conv(wa_ref, cwa_ref, cba_ref) * _silu(conv(wg_ref, cwg_ref, cbg_ref))
    acc_ref[...] += jnp.dot(act.astype(BF16), wd_ref[...], preferred_element_type=F32)

    @pl.when(f == pl.num_programs(2) - 1)
    def _():
        o_ref[...] = x_ref[...] + acc_ref[...]


def _ffn(x3, g_ffn, w_up, conv_w, conv_b, w_down):
    b, l, _ = x3.shape
    tm = min(FFN_TM, l)
    nf = D_FF // FFN_TF
    hb = tm // FFN_HALO
    wcol = lambda r, j0: pl.BlockSpec((r, FFN_TF), lambda i, m, f: (0, j0 + f))
    return pl.pallas_call(
        _ffn_kernel,
        grid=(b, l // tm, nf),
        in_specs=[
            pl.BlockSpec((None, FFN_HALO, D_MODEL), lambda i, m, f: (i, jnp.maximum(m * hb - 1, 0), 0)),
            pl.BlockSpec((None, tm, D_MODEL), lambda i, m, f: (i, m, 0)),
            pl.BlockSpec((None, FFN_HALO, D_MODEL),
                         lambda i, m, f: (i, jnp.minimum((m + 1) * hb, l // FFN_HALO - 1), 0)),
            pl.BlockSpec((1, D_MODEL), lambda i, m, f: (0, 0)),
            wcol(D_MODEL, 0), wcol(D_MODEL, nf),
            wcol(FFN_CONV, 0), wcol(FFN_CONV, nf),
            wcol(1, 0), wcol(1, nf),
            pl.BlockSpec((FFN_TF, D_MODEL), lambda i, m, f: (f, 0)),
        ],
        out_specs=pl.BlockSpec((None, tm, D_MODEL), lambda i, m, f: (i, m, 0)),
        out_shape=jax.ShapeDtypeStruct((b, l, D_MODEL), F32),
        scratch_shapes=[pltpu.VMEM((tm + 2 * FFN_HALO, D_MODEL), BF16), pltpu.VMEM((tm, D_MODEL), F32)],
        compiler_params=_cparams(3),
        name="ffn",
    )(x3, x3, x3, g_ffn, w_up, w_up, conv_w, conv_w, conv_b, conv_b, w_down)


def _layer(x, bias_tab, p):
    b, l, _ = x.shape
    x2 = x.reshape(b * l, D_MODEL)
    packed2, dt2, kt2 = _proj(x2, p["g_mix"], p["w_main"], p["w_dt"], p["dt_bias"], p["w_kt"], p["k_norm_g"])
    packed3 = packed2.reshape(b, l, PROJ_W)
    att = _attention(packed3, kt2, p["q_norm_g"], bias_tab)
    xbc = _ssd_conv(packed3, p["ssd_conv_w"], p["ssd_conv_b"])
    pcs, acsr = _ssd_prep(dt2.reshape(b, l, NDT), p["a_log"])
    y = _ssd(xbc, packed3, pcs, acsr, p["d_skip_e"], p["ssd_norm_g"])
    x1 = _merge(att.reshape(b * l, ATT_W), y.reshape(b * l, SSD_INNER), packed2, x2,
                p["b_gate"], p["w_att_out"], p["w_ssd_out"], p["w_o"])
    return _ffn(x1.reshape(b, l, D_MODEL), p["g_ffn"], p["w_up"], p["ffn_conv_w"], p["ffn_conv_b"],
                p["w_down"])


def kernel(x_prompt, x_sample, g_mix, w_in, b_gate, q_norm_g, k_norm_g, rel_bias, ssd_conv_w, ssd_conv_b,
           dt_bias_f, dt_bias_b, A_log_f, A_log_b, D_skip, ssd_norm_g, w_att_out, w_ssd_out, w_o, g_ffn,
           w_up, ffn_conv_w, ffn_conv_b, w_down):
    depth = g_mix.shape[0]
    xs = [x_prompt, x_sample]
    perm = jnp.asarray(_group_major_perm(), jnp.int32)
    for layer in range(depth):
        w = w_in[layer]
        p = {
            "g_mix": g_mix[layer][None, :],
            "w_main": jnp.concatenate([w[:, :ATT_W], w[:, 2 * ATT_W:DT_COL], w[:, DT_COL + 2 * SSD_HEADS:]],
                                      axis=1).astype(BF16),
            "w_kt": w[:, ATT_W:2 * ATT_W].T.astype(BF16),
            "w_dt": w[:, DT_COL:DT_COL + 2 * SSD_HEADS][:, perm].astype(BF16),
            "dt_bias": jnp.concatenate([dt_bias_f[layer], dt_bias_b[layer]])[perm][None, :],
            "q_norm_g": q_norm_g[layer],
            "k_norm_g": k_norm_g[layer],
            "ssd_conv_w": ssd_conv_w[layer],
            "ssd_conv_b": ssd_conv_b[layer][None, :],
            "a_log": jnp.concatenate([A_log_f[layer], A_log_b[layer]])[perm][None, :],
            "d_skip_e": jnp.repeat(D_skip[layer], SSD_HEAD_DIM)[None, :],
            "ssd_norm_g": ssd_norm_g[layer][None, :],
            "b_gate": b_gate[layer][None, :],
            "w_att_out": w_att_out[layer].astype(BF16),
            "w_ssd_out": w_ssd_out[layer].astype(BF16),
            "w_o": w_o[layer].astype(BF16),
            "g_ffn": g_ffn[layer][None, :],
            "w_up": w_up[layer].astype(BF16),
            "ffn_conv_w": ffn_conv_w[layer],
            "ffn_conv_b": ffn_conv_b[layer][None, :],
            "w_down": w_down[layer].astype(BF16),
        }
        bias_tab = _bias_table(rel_bias[layer])
        xs = [_layer(x, bias_tab, p) for x in xs]
    return tuple(xs)
```

```python
import functools

import jax
import jax.numpy as jnp
from jax import lax
from jax.experimental import pallas as pl
from jax.experimental.pallas import tpu as pltpu

F32 = jnp.float32
BF16 = jnp.bfloat16
HIGHEST = lax.Precision.HIGHEST

D_MODEL = 1024
GRID_W = 64
WIN_R = 8
WIN_C = 16
ATT_HEADS = 16
HEAD_DIM = 64
ATT_W = ATT_HEADS * HEAD_DIM
SSD_INNER = 2 * D_MODEL
SSD_HEAD_DIM = 64
SSD_HEADS = SSD_INNER // SSD_HEAD_DIM
SSD_GROUPS = 8
SSD_HPG = SSD_HEADS // SSD_GROUPS
SSD_STATE = 128
SSD_CONV = 5
CONV_DIM = SSD_INNER + 2 * SSD_GROUPS * SSD_STATE
CHUNK = 128
D_FF = 2816
FFN_CONV = 3
EPS = 1e-6

OFF_Q = 0
OFF_V = OFF_Q + ATT_W
OFF_Z = OFF_V + ATT_W
OFF_XBC = OFF_Z + SSD_INNER
OFF_GA = OFF_XBC + CONV_DIM
OFF_GS = OFF_GA + D_MODEL
PROJ_W = OFF_GS + D_MODEL
DT_COL = 3 * ATT_W + SSD_INNER + CONV_DIM

NEG = -1e30
LOG2E = 1.4426950408889634
VMEM_LIMIT = 56 * 1024 * 1024


def _cparams(n_axes):
    return pltpu.CompilerParams(
        dimension_semantics=("arbitrary",) * n_axes, vmem_limit_bytes=VMEM_LIMIT)


def _rms(x, g):
    return x * lax.rsqrt(jnp.mean(x * x, axis=-1, keepdims=True) + EPS) * g


def _silu(x):
    return x * jax.nn.sigmoid(x)


PROJ_TM = 1024
PROJ_TN = 2048


PROJ_NT = PROJ_W // PROJ_TN
KT_COLS = 256


def _proj_kernel(x_ref, g_ref, w_ref, wdt_ref, dtb_ref, wkt_ref, gk_ref, o_ref, dt_ref, kt_ref, h_ref):
    j = pl.program_id(1)

    @pl.when(j == 0)
    def _():
        h = _rms(x_ref[...], g_ref[...]).astype(BF16)
        h_ref[...] = h
        raw = jnp.dot(h, wdt_ref[...], preferred_element_type=F32)
        dt_ref[...] = jax.nn.softplus(raw + dtb_ref[...])

    @pl.when(j < PROJ_NT)
    def _():
        o_ref[...] = jnp.dot(h_ref[...], w_ref[...], preferred_element_type=F32).astype(BF16)

    @pl.when(j == PROJ_NT)
    def _():
        gk = gk_ref[...].reshape(1, HEAD_DIM, 1)
        for c in range(h_ref.shape[0] // KT_COLS):
            cols = slice(c * KT_COLS, (c + 1) * KT_COLS)
            kt = lax.dot_general(wkt_ref[...], h_ref[cols, :], (((1,), (1,)), ((), ())),
                                 preferred_element_type=F32)
            kt = kt.reshape(ATT_HEADS, HEAD_DIM, KT_COLS)
            ms = jnp.mean(kt * kt, axis=1, keepdims=True)
            kt_ref[:, cols] = (kt * lax.rsqrt(ms + EPS) * gk).reshape(ATT_W, KT_COLS).astype(BF16)


def _proj(x2, g_mix, w_main, w_dt, dt_bias, w_kt, k_norm_g):
    m = x2.shape[0]
    tm = min(PROJ_TM, m)
    last = PROJ_NT - 1
    return pl.pallas_call(
        _proj_kernel,
        grid=(m // tm, PROJ_NT + 1),
        in_specs=[
            pl.BlockSpec((tm, D_MODEL), lambda i, j: (i, 0)),
            pl.BlockSpec((1, D_MODEL), lambda i, j: (0, 0)),
            pl.BlockSpec((D_MODEL, PROJ_TN), lambda i, j: (0, jnp.minimum(j, last))),
            pl.BlockSpec((D_MODEL, 2 * SSD_HEADS), lambda i, j: (0, 0)),
            pl.BlockSpec((1, 2 * SSD_HEADS), lambda i, j: (0, 0)),
            pl.BlockSpec((ATT_W, D_MODEL), lambda i, j: (0, 0)),
            pl.BlockSpec((HEAD_DIM, 1), lambda i, j: (0, 0)),
        ],
        out_specs=[
            pl.BlockSpec((tm, PROJ_TN), lambda i, j: (i, jnp.minimum(j, last))),
            pl.BlockSpec((tm, 2 * SSD_HEADS), lambda i, j: (i, 0)),
            pl.BlockSpec((ATT_W, tm), lambda i, j: (0, i)),
        ],
        out_shape=[
            jax.ShapeDtypeStruct((m, PROJ_W), BF16),
            jax.ShapeDtypeStruct((m, 2 * SSD_HEADS), F32),
            jax.ShapeDtypeStruct((ATT_W, m), BF16),
        ],
        scratch_shapes=[pltpu.VMEM((tm, D_MODEL), BF16)],
        compiler_params=_cparams(2),
        name="proj",
    )(x2, g_mix, w_main, w_dt, dt_bias, w_kt, k_norm_g.reshape(HEAD_DIM, 1))


N_ROFF = 2 * WIN_R - 1
N_COFF = 2 * WIN_C - 1
KEYS = WIN_R * GRID_W


def _bias_kernel(rb_ref, o_ref):
    h = pl.program_id(0)
    c = lax.broadcasted_iota(jnp.int32, (GRID_W, GRID_W), 0)
    cp = lax.broadcasted_iota(jnp.int32, (GRID_W, GRID_W), 1)
    cs = jnp.clip(c - WIN_C // 2, 0, GRID_W - WIN_C)
    inside = jnp.where(cp >= cs, jnp.where(cp < cs + WIN_C, 1, 0), 0)
    idx = jnp.clip(cp - c + WIN_C - 1, 0, N_COFF - 1)
    tiles = []
    for ro in range(N_ROFF):
        base = (h * N_ROFF + ro) * N_COFF

        def body(k, acc, base=base):
            return jnp.where(idx == k, rb_ref[base + k], acc)

        t = lax.fori_loop(0, N_COFF, body, jnp.zeros((GRID_W, GRID_W), F32))
        tiles.append(jnp.where(inside == 1, t, NEG))
    for d in range(WIN_R):
        for i in range(WIN_R):
            o_ref[0, d, :, i * GRID_W:(i + 1) * GRID_W] = tiles[i - d + WIN_R - 1]


def _bias_table(rel_bias):
    return pl.pallas_call(
        _bias_kernel,
        grid=(ATT_HEADS,),
        in_specs=[pl.BlockSpec(memory_space=pltpu.SMEM)],
        out_specs=pl.BlockSpec((1, WIN_R, GRID_W, KEYS), lambda h: (h // 2, 0, h % 2, 0)),
        out_shape=jax.ShapeDtypeStruct((ATT_HEADS // 2, WIN_R, 2 * GRID_W, KEYS), F32),
        compiler_params=_cparams(1),
        name="bias_table",
    )(rel_bias.reshape(-1))


NORM_ROWS = 512


ATT_ROWS = 8
SHIFT_COLS = 256


def _aligned(x, m):
    return x if isinstance(x, int) else pl.multiple_of(x, m)


def _attn_kernel(q_ref, kt_in_ref, v_ref, gq_ref, bias_ref, o_ref, qn_ref, kt_ref, vx_ref, s_ref, p_ref, *, rows):
    l = rows * GRID_W
    pair = 2 * HEAD_DIM
    lane = lax.broadcasted_iota(jnp.int32, (2 * pair, pair), 0) % pair // HEAD_DIM
    lane_t = lax.broadcasted_iota(jnp.int32, (2 * pair, pair), 1) // HEAD_DIM
    seg = jnp.where(lane == lane_t, 1.0 / HEAD_DIM, 0.0).astype(BF16)

    def norm_body(i, carry):
        sl = pl.ds(pl.multiple_of(i * NORM_ROWS, NORM_ROWS), NORM_ROWS)
        x = q_ref[sl, :].astype(F32)
        sq = x * x
        hi = sq.astype(BF16)
        lo = (sq - hi.astype(F32)).astype(BF16)
        ms = jnp.dot(jnp.concatenate([hi, lo], axis=-1), seg, preferred_element_type=F32)
        qn_ref[sl, :] = (x * lax.rsqrt(ms + EPS) * gq_ref[...] * (HEAD_DIM ** -0.5)).astype(BF16)
        vx_ref[sl, 0:pair] = v_ref[sl, :]
        vx_ref[sl, pair:2 * pair] = jnp.ones((NORM_ROWS, pair), BF16)
        return carry

    lax.fori_loop(0, l // NORM_ROWS, norm_body, 0)

    ext = SHIFT_COLS + 2 * GRID_W

    def shifted(c0):
        words = pltpu.bitcast(kt_in_ref[:, pl.ds(c0, ext)], jnp.uint32)
        return pltpu.bitcast(pltpu.roll(words, ext - GRID_W, 1), BF16)

    def shift_body(i, carry):
        c0 = pl.multiple_of(i * SHIFT_COLS, SHIFT_COLS)
        kt_ref[0, :, pl.ds(c0, SHIFT_COLS)] = kt_in_ref[:, pl.ds(c0, SHIFT_COLS)]
        kt_ref[1, :, pl.ds(c0, SHIFT_COLS)] = shifted(c0)[:, :SHIFT_COLS]
        return carry

    n_full = (l - ext) // SHIFT_COLS + 1
    lax.fori_loop(0, n_full, shift_body, 0)
    for c0 in range(n_full * SHIFT_COLS, l, SHIFT_COLS):
        kt_ref[0, :, c0:c0 + SHIFT_COLS] = kt_in_ref[:, c0:c0 + SHIFT_COLS]
    kt_ref[1, :, l - SHIFT_COLS:l - GRID_W] = shifted(l - ext)[:, 2 * GRID_W:ext - GRID_W]

    top = lax.broadcasted_iota(jnp.int32, (1, pair), 1) < HEAD_DIM

    def window(r):
        if isinstance(r, int):
            rs = min(max(r - WIN_R // 2, 0), rows - WIN_R)
            return rs, r - rs, rs % 2
        rs = jnp.clip(r - WIN_R // 2, 0, rows - WIN_R)
        return rs, r - rs, lax.rem(rs, 2)

    def scores_stage(r, slot):
        rs, d, odd = window(r)
        q2 = qn_ref[pl.ds(_aligned(r * GRID_W, GRID_W), GRID_W), :]
        zero = jnp.zeros_like(q2)
        qq = jnp.concatenate([jnp.where(top, q2, zero), jnp.where(top, zero, q2)], axis=0)
        kcol = pl.ds(_aligned((rs - odd) * GRID_W, 2 * GRID_W), KEYS)
        s_ref[slot] = jnp.dot(qq, kt_ref[odd, :, kcol], preferred_element_type=F32) + bias_ref[d]

    def softmax_stage(slot):
        s = s_ref[slot]
        p_ref[slot] = jnp.exp(s - jnp.max(s, axis=-1, keepdims=True)).astype(BF16)

    def values_stage(r, slot):
        rs, _, _ = window(r)
        vrow = pl.ds(_aligned(rs * GRID_W, GRID_W), KEYS)
        ov = jnp.dot(p_ref[slot], vx_ref[vrow, :], preferred_element_type=F32)
        num = jnp.where(top, ov[:GRID_W, :pair], ov[GRID_W:, :pair])
        den = jnp.where(top, ov[:GRID_W, pair:], ov[GRID_W:, pair:])
        o_ref[pl.ds(_aligned(r * GRID_W, GRID_W), GRID_W), :] = (num / den).astype(BF16)

    n_groups = rows // ATT_ROWS

    def stages(t, par):
        static = isinstance(t, int)
        mine, other = par * ATT_ROWS, (1 - par) * ATT_ROWS
        for u in range(ATT_ROWS):
            if not static or 0 <= t - 2 < n_groups:
                values_stage((t - 2) * ATT_ROWS + u, mine + u)
        for u in range(ATT_ROWS):
            if not static or 0 <= t - 1 < n_groups:
                softmax_stage(other + u)
        for u in range(ATT_ROWS):
            if not static or 0 <= t < n_groups:
                scores_stage(t * ATT_ROWS + u, mine + u)

    stages(0, 0)
    stages(1, 1)

    def body(k, carry):
        stages(2 * k, 0)
        stages(2 * k + 1, 1)
        return carry

    lax.fori_loop(1, n_groups // 2, body, 0)
    stages(n_groups, 0)
    stages(n_groups + 1, 1)


def _attention(packed3, kt2, q_norm_g, bias_tab):
    b, l, _ = packed3.shape
    rows = l // GRID_W
    assert rows % (2 * ATT_ROWS) == 0 and l % SHIFT_COLS == 0
    pair = 2 * HEAD_DIM
    gq = jnp.tile(q_norm_g.reshape(1, HEAD_DIM), (1, 2))
    col = lambda off: (lambda i, j: (i, 0, off // pair + j))
    return pl.pallas_call(
        functools.partial(_attn_kernel, rows=rows),
        grid=(b, ATT_HEADS // 2),
        in_specs=[
            pl.BlockSpec((None, l, pair), col(OFF_Q)),
            pl.BlockSpec((pair, l), lambda i, j: (j, i)),
            pl.BlockSpec((None, l, pair), col(OFF_V)),
            pl.BlockSpec((1, pair), lambda i, j: (0, 0)),
            pl.BlockSpec((None, WIN_R, 2 * GRID_W, KEYS), lambda i, j: (j, 0, 0, 0)),
        ],
        out_specs=pl.BlockSpec((None, l, pair), lambda i, j: (i, 0, j)),
        out_shape=jax.ShapeDtypeStruct((b, l, ATT_W), BF16),
        scratch_shapes=[pltpu.VMEM((l, pair), BF16), pltpu.VMEM((2, pair, l), BF16),
                        pltpu.VMEM((l, 2 * pair), BF16), pltpu.VMEM((2 * ATT_ROWS, 2 * GRID_W, KEYS), F32),
                        pltpu.VMEM((2 * ATT_ROWS, 2 * GRID_W, KEYS), BF16)],
        compiler_params=_cparams(2),
        name="attention",
    )(packed3, kt2, packed3, gq, bias_tab)


CONV_TN = 512
CONV_ROWS = 128
CONV_UNROLL = 8
HALO = 16


def _roll_rows(x, offset, lo, n):
    if offset == 0:
        return x[lo:lo + n]
    return pltpu.roll(x, (-offset) % x.shape[0], 0)[lo:lo + n]


CONV_TAPS = [k for k in range(SSD_CONV) if k != SSD_CONV // 2]


def _conv_shift_matrix():
    half = SSD_CONV // 2
    ext_rows = CONV_ROWS + 2 * HALO
    ri = lax.broadcasted_iota(jnp.int32, (len(CONV_TAPS) * CONV_ROWS, ext_rows), 0)
    ci = lax.broadcasted_iota(jnp.int32, (len(CONV_TAPS) * CONV_ROWS, ext_rows), 1)
    src = ri % CONV_ROWS + HALO - half + ri // CONV_ROWS
    src = jnp.where(ri // CONV_ROWS >= half, src + 1, src)
    return jnp.where(ci == src, 1.0, 0.0).astype(BF16)


def _conv_step(x_ref, w_ref, b_ref, o_ref, shift, c):
    l = x_ref.shape[0]
    half = SSD_CONV // 2
    lo = c * CONV_ROWS
    if isinstance(c, int):
        lo_before, lo_after = max(lo - HALO, 0), min(lo + CONV_ROWS, l - HALO)
    else:
        lo_before, lo_after = jnp.maximum(lo - HALO, 0), jnp.minimum(lo + CONV_ROWS, l - HALO)
    main = x_ref[pl.ds(_aligned(lo, CONV_ROWS), CONV_ROWS), :]
    before = x_ref[pl.ds(_aligned(lo_before, HALO), HALO), :]
    after = x_ref[pl.ds(_aligned(lo_after, HALO), HALO), :]
    zero = jnp.zeros_like(before)
    ext = jnp.concatenate([jnp.where(c > 0, before, zero), main,
                           jnp.where(c < l // CONV_ROWS - 1, after, zero)], axis=0)
    sh = jnp.dot(shift, ext, preferred_element_type=F32)
    acc = b_ref[...] + main.astype(F32) * w_ref[half:half + 1, :]
    for t, k in enumerate(CONV_TAPS):
        acc = acc + sh[t * CONV_ROWS:(t + 1) * CONV_ROWS] * w_ref[k:k + 1, :]
    o_ref[pl.ds(_aligned(lo, CONV_ROWS), CONV_ROWS), :] = _silu(acc).astype(BF16)


def _conv_kernel(x_ref, w_ref, b_ref, o_ref):
    shift = _conv_shift_matrix()

    def body(t, carry):
        for u in range(CONV_UNROLL):
            _conv_step(x_ref, w_ref, b_ref, o_ref, shift, CONV_UNROLL * t + u)
        return carry

    lax.fori_loop(0, x_ref.shape[0] // (CONV_UNROLL * CONV_ROWS), body, 0)


def _ssd_conv(packed3, conv_w, conv_b):
    b, l, _ = packed3.shape
    assert l % (CONV_UNROLL * CONV_ROWS) == 0
    c0 = OFF_XBC // CONV_TN
    return pl.pallas_call(
        _conv_kernel,
        grid=(b, CONV_DIM // CONV_TN),
        in_specs=[
            pl.BlockSpec((None, l, CONV_TN), lambda i, j: (i, 0, c0 + j)),
            pl.BlockSpec((SSD_CONV, CONV_TN), lambda i, j: (0, j)),
            pl.BlockSpec((1, CONV_TN), lambda i, j: (0, j)),
        ],
        out_specs=pl.BlockSpec((None, l, CONV_TN), lambda i, j: (i, 0, j)),
        out_shape=jax.ShapeDtypeStruct((b, l, CONV_DIM), BF16),
        compiler_params=_cparams(2),
        name="ssd_conv",
    )(packed3, conv_w, conv_b)


GW = SSD_HPG * SSD_HEAD_DIM


def _split3(v):
    hi = v.astype(BF16)
    r1 = v - hi.astype(F32)
    mid = r1.astype(BF16)
    lo = (r1 - mid.astype(F32)).astype(BF16)
    return hi, mid, lo


ND = 2 * SSD_HPG
NDT = 2 * SSD_HEADS
PIECE_W = 6 * NDT
SPREAD_W = SSD_HPG * CHUNK + GW


def _group_major_perm():
    return [d * SSD_HEADS + g * SSD_HPG + r for g in range(SSD_GROUPS) for d in range(2) for r in range(SSD_HPG)]


def _piece_layout():
    import numpy as np
    m = np.zeros((PIECE_W, SSD_GROUPS * CHUNK), np.float32)
    for q in range(6):
        for g in range(SSD_GROUPS):
            for k in range(ND):
                m[q * NDT + ND * g + k, CHUNK * g + ND * q + k] = 1.0
    return jnp.asarray(m, BF16)


def _spread_layout():
    import numpy as np
    m = np.zeros((2, CHUNK, SPREAD_W), np.float32)
    for d in range(2):
        for q in range(6):
            for r in range(SSD_HPG):
                row = ND * q + SSD_HPG * d + r
                if q < 3:
                    m[d, row, CHUNK * r:CHUNK * (r + 1)] = 1.0
                else:
                    m[d, row, SSD_HPG * CHUNK + SSD_HEAD_DIM * r:SSD_HPG * CHUNK + SSD_HEAD_DIM * (r + 1)] = 1.0
    return jnp.asarray(m, BF16)


def _prep_kernel(dt_ref, alog_ref, lay_ref, pcs_ref, acsr_ref):
    neg_a = -jnp.exp(alog_ref[...]) * LOG2E
    ii = lax.broadcasted_iota(jnp.int32, (CHUNK, CHUNK), 0)
    jj = lax.broadcasted_iota(jnp.int32, (CHUNK, CHUNK), 1)
    tri_f = jnp.where(jj <= ii, 1.0, 0.0).astype(BF16)
    tri_b = jnp.where(jj >= ii, 1.0, 0.0).astype(BF16)
    col = lax.broadcasted_iota(jnp.int32, (1, 3 * NDT), 1)
    is_fwd = (col % ND) < SSD_HPG
    er = lax.broadcasted_iota(jnp.int32, (NDT, NDT), 0)
    ec = lax.broadcasted_iota(jnp.int32, (NDT, NDT), 1)
    eye = jnp.where(er == ec, 1.0, 0.0).astype(BF16)
    for c in range(PREP_CHUNKS):
        rows = slice(c * CHUNK, (c + 1) * CHUNK)
        dt = dt_ref[rows, :]
        a3 = jnp.concatenate(_split3(dt * neg_a), axis=-1)
        s3 = jnp.where(is_fwd, jnp.dot(tri_f, a3, preferred_element_type=F32),
                       jnp.dot(tri_b, a3, preferred_element_type=F32))
        acs = s3[:, 0:NDT] + s3[:, NDT:2 * NDT] + s3[:, 2 * NDT:3 * NDT]
        acs3 = _split3(acs)
        pieces = jnp.concatenate(acs3 + _split3(dt), axis=-1)
        pcs_ref[rows, :] = jnp.dot(pieces, lay_ref[...], preferred_element_type=F32).astype(BF16)
        tr = None
        for p in acs3:
            d = lax.dot_general(eye, p, (((1,), (1,)), ((), ())), preferred_element_type=F32)
            tr = d if tr is None else tr + d
        acsr_ref[c] = tr


PREP_CHUNKS = 8


def _ssd_prep(dt3, a_log):
    b, l, _ = dt3.shape
    nc = l // CHUNK
    assert nc % PREP_CHUNKS == 0
    rows = PREP_CHUNKS * CHUNK
    return pl.pallas_call(
        _prep_kernel,
        grid=(b, nc // PREP_CHUNKS),
        in_specs=[
            pl.BlockSpec((None, rows, NDT), lambda i, c: (i, c, 0)),
            pl.BlockSpec((1, NDT), lambda i, c: (0, 0)),
            pl.BlockSpec((PIECE_W, SSD_GROUPS * CHUNK), lambda i, c: (0, 0)),
        ],
        out_specs=[
            pl.BlockSpec((None, rows, SSD_GROUPS * CHUNK), lambda i, c: (i, c, 0)),
            pl.BlockSpec((None, PREP_CHUNKS, NDT, CHUNK), lambda i, c: (i, c, 0, 0)),
        ],
        out_shape=[
            jax.ShapeDtypeStruct((b, l, SSD_GROUPS * CHUNK), BF16),
            jax.ShapeDtypeStruct((b, nc, NDT, CHUNK), F32),
        ],
        compiler_params=_cparams(2),
        name="ssd_prep",
    )(dt3, a_log, _piece_layout())


def _ssd_kernel(x_ref, b_ref, c_ref, z_ref, pcs_ref, acsr_ref, lay_ref, dskip_ref, ng_ref, o_ref,
                yp_ref, st_ref, g_ref, xdt_ref, xw_ref, ea_ref, et_ref, *, nc):
    ii = lax.broadcasted_iota(jnp.int32, (CHUNK, CHUNK), 0)
    jj = lax.broadcasted_iota(jnp.int32, (CHUNK, CHUNK), 1)
    keep_f = jj <= ii
    keep_b = jj >= ii
    low_half = lax.broadcasted_iota(jnp.int32, (1, CHUNK), 1) < SSD_HEAD_DIM
    head_of_lane = lax.broadcasted_iota(jnp.int32, (1, GW), 1) // SSD_HEAD_DIM
    st_ref[...] = jnp.zeros_like(st_ref)

    def chunk_rows(c):
        return pl.ds(_aligned(c * CHUNK, CHUNK), CHUNK)

    def decay_stage(c, d, par):
        slot = 2 * par + d
        keep = keep_b if d else keep_f
        rows = chunk_rows(c)
        sp = jnp.dot(pcs_ref[rows, :], lay_ref[d], preferred_element_type=F32)
        acs_w = [sp[:, r * CHUNK:(r + 1) * CHUNK] for r in range(SSD_HPG)]
        dt_e = sp[:, SSD_HPG * CHUNK:]
        acs_e = jnp.concatenate([jnp.where(low_half, acs_w[0], acs_w[1]),
                                 jnp.where(low_half, acs_w[2], acs_w[3])], axis=-1)
        acs_r = acsr_ref[c]
        xdt = x_ref[rows, :].astype(F32) * dt_e
        scores = lax.dot_general(c_ref[rows, :], b_ref[rows, :], (((1,), (1,)), ((), ())),
                                 preferred_element_type=F32)
        for r in range(SSD_HPG):
            seg = acs_w[r] - acs_r[SSD_HPG * d + r:SSD_HPG * d + r + 1, :]
            g_ref[slot, :, r * CHUNK:(r + 1) * CHUNK] = (scores * jnp.exp2(jnp.where(keep, seg, NEG))).astype(BF16)
        total = acs_e[0:1, :] if d else acs_e[CHUNK - 1:CHUNK, :]
        xdt_ref[slot] = xdt.astype(BF16)
        xw_ref[slot] = (xdt * jnp.exp2(total - acs_e)).astype(BF16)
        ea_ref[slot] = jnp.exp2(acs_e)
        et_ref[slot] = jnp.exp2(total)

    def state_stage(c, d, par):
        slot = 2 * par + d
        rows = chunk_rows(c)
        xdt_b = xdt_ref[slot]
        zero = jnp.zeros_like(xdt_b)
        xbd = jnp.concatenate([jnp.where(head_of_lane == r, xdt_b, zero) for r in range(SSD_HPG)], axis=0)
        cm = c_ref[rows, :]
        st = st_ref[d]
        y = (jnp.dot(g_ref[slot], xbd, preferred_element_type=F32)
             + jnp.dot(cm, st.astype(BF16), preferred_element_type=F32) * ea_ref[slot])
        st_ref[d] = st * et_ref[slot] + lax.dot_general(
            b_ref[rows, :], xw_ref[slot], (((0,), (0,)), ((), ())), preferred_element_type=F32)
        return rows, y

    def first_visit(c, d, par):
        rows, y = state_stage(c, d, par)
        yp_ref[rows, :] = y

    def second_visit(c, d, par):
        rows, y = state_stage(c, d, par)
        xf = x_ref[rows, :].astype(F32)
        yt = (yp_ref[rows, :] + y + xf * dskip_ref[...]) * _silu(z_ref[rows, :].astype(F32))
        o_ref[rows, :] = _rms(yt, ng_ref[...]).astype(BF16)

    def step(t, par, visit):
        visit(t, 0, par)
        visit(nc - 1 - t, 1, par)
        decay_stage(t + 1, 0, 1 - par)
        decay_stage(nc - 2 - t, 1, 1 - par)

    def four_steps(visit, t0):
        def body(k, carry):
            for u in range(4):
                step(t0 + 4 * k + u, u % 2, visit)
            return carry
        return body

    decay_stage(0, 0, 0)
    decay_stage(nc - 1, 1, 0)
    lax.fori_loop(0, nc // 8, four_steps(first_visit, 0), 0)
    lax.fori_loop(0, nc // 8 - 1, four_steps(second_visit, nc // 2), 0)
    step(nc - 4, 0, second_visit)
    step(nc - 3, 1, second_visit)
    step(nc - 2, 0, second_visit)
    second_visit(nc - 1, 0, 1)
    second_visit(0, 1, 1)


def _ssd(xbc3, packed3, pcs3, acsr4, d_skip_e, norm_g):
    b, l, _ = xbc3.shape
    nc = l // CHUNK
    assert nc % 8 == 0
    b0 = SSD_INNER // SSD_STATE
    seq = lambda w, j0: pl.BlockSpec((None, l, w), lambda i, g: (i, 0, j0 + g))
    return pl.pallas_call(
        functools.partial(_ssd_kernel, nc=nc),
        grid=(b, SSD_GROUPS),
        in_specs=[
            seq(GW, 0), seq(SSD_STATE, b0), seq(SSD_STATE, b0 + SSD_GROUPS), seq(GW, OFF_Z // GW),
            seq(CHUNK, 0),
            pl.BlockSpec((None, nc, ND, CHUNK), lambda i, g: (i, 0, g, 0)),
            pl.BlockSpec((2, CHUNK, SPREAD_W), lambda i, g: (0, 0, 0)),
            pl.BlockSpec((1, GW), lambda i, g: (0, g)),
            pl.BlockSpec((1, GW), lambda i, g: (0, g)),
        ],
        out_specs=seq(GW, 0),
        out_shape=jax.ShapeDtypeStruct((b, l, SSD_INNER), BF16),
        scratch_shapes=[
            pltpu.VMEM((l, GW), F32),
            pltpu.VMEM((2, SSD_STATE, GW), F32),
            pltpu.VMEM((4, CHUNK, SSD_HPG * CHUNK), BF16),
            pltpu.VMEM((4, CHUNK, GW), BF16),
            pltpu.VMEM((4, CHUNK, GW), BF16),
            pltpu.VMEM((4, CHUNK, GW), F32),
            pltpu.VMEM((4, 1, GW), F32),
        ],
        compiler_params=_cparams(2),
        name="ssd",
    )(xbc3, xbc3, xbc3, packed3, pcs3, acsr4, _spread_layout(), d_skip_e, norm_g)


MERGE_TM = 1024


def _merge_kernel(att_ref, y_ref, ga_ref, gs_ref, x_ref, ba_ref, bs_ref, wa_ref, ws_ref, wo_ref, o_ref):
    m1 = jnp.dot(att_ref[...], wa_ref[...], preferred_element_type=F32)
    m2 = jnp.dot(y_ref[...], ws_ref[...], preferred_element_type=F32)
    merged = (jax.nn.sigmoid(ga_ref[...].astype(F32) + ba_ref[...]) * m1
              + jax.nn.sigmoid(gs_ref[...].astype(F32) + bs_ref[...]) * m2)
    o_ref[...] = x_ref[...] + jnp.dot(merged.astype(BF16), wo_ref[...], preferred_element_type=F32)


def _merge(att2, y2, packed2, x2, b_gate, wa, ws, wo):
    m = x2.shape[0]
    tm = min(MERGE_TM, m)
    row = lambda w, j=0: pl.BlockSpec((tm, w), lambda i: (i, j))
    full = lambda r, c, j=0: pl.BlockSpec((r, c), lambda i: (0, j))
    return pl.pallas_call(
        _merge_kernel,
        grid=(m // tm,),
        in_specs=[
            row(ATT_W), row(SSD_INNER), row(D_MODEL, OFF_GA // D_MODEL), row(D_MODEL, OFF_GS // D_MODEL),
            row(D_MODEL), full(1, D_MODEL, 0), full(1, D_MODEL, 1),
            full(ATT_W, D_MODEL), full(SSD_INNER, D_MODEL), full(D_MODEL, D_MODEL),
        ],
        out_specs=row(D_MODEL),
        out_shape=jax.ShapeDtypeStruct((m, D_MODEL), F32),
        compiler_params=_cparams(1),
        name="merge",
    )(att2, y2, packed2, packed2, x2, b_gate, b_gate, wa, ws, wo)


FFN_TM = 1024
FFN_TF = 1408
FFN_HALO = 8


def _ffn_kernel(prev_ref, x_ref, next_ref, g_ref, wa_ref, wg_ref, cwa_ref, cwg_ref, cba_ref, cbg_ref,
                wd_ref, o_ref, h_ref, acc_ref):
    m = pl.program_id(1)
    f = pl.program_id(2)
    tm = x_ref.shape[0]

    @pl.when(f == 0)
    def _():
        prev = jnp.where(m > 0, prev_ref[...], 0.0)
        nxt = jnp.where(m < pl.num_programs(1) - 1, next_ref[...], 0.0)
        ext = jnp.concatenate([prev, x_ref[...], nxt], axis=0)
        h_ref[...] = _rms(ext, g_ref[...]).astype(BF16)
        acc_ref[...] = jnp.zeros_like(acc_ref)

    h = h_ref[...]

    def conv(w_ref, cw_ref, cb_ref):
        u = jnp.dot(h, w_ref[...], preferred_element_type=F32)
        out = jnp.broadcast_to(cb_ref[...], (tm, u.shape[1]))
        for k in range(FFN_CONV):
            out = out + _roll_rows(u, k - FFN_CONV // 2, FFN_HALO, tm) * cw_ref[k:k + 1, :]
        return out

    act = conv(wa_ref, cwa_ref, cba_ref) * _silu(conv(wg_ref, cwg_ref, cbg_ref))
    acc_ref[...] += jnp.dot(act.astype(BF16), wd_ref[...], preferred_element_type=F32)

    @pl.when(f == pl.num_programs(2) - 1)
    def _():
        o_ref[...] = x_ref[...] + acc_ref[...]


def _ffn(x3, g_ffn, w_up, conv_w, conv_b, w_down):
    b, l, _ = x3.shape
    tm = min(FFN_TM, l)
    nf = D_FF // FFN_TF
    hb = tm // FFN_HALO
    wcol = lambda r, j0: pl.BlockSpec((r, FFN_TF), lambda i, m, f: (0, j0 + f))
    return pl.pallas_call(
        _ffn_kernel,
        grid=(b, l // tm, nf),
        in_specs=[
            pl.BlockSpec((None, FFN_HALO, D_MODEL), lambda i, m, f: (i, jnp.maximum(m * hb - 1, 0), 0)),
            pl.BlockSpec((None, tm, D_MODEL), lambda i, m, f: (i, m, 0)),
            pl.BlockSpec((None, FFN_HALO, D_MODEL),
                         lambda i, m, f: (i, jnp.minimum((m + 1) * hb, l // FFN_HALO - 1), 0)),
            pl.BlockSpec((1, D_MODEL), lambda i, m, f: (0, 0)),
            wcol(D_MODEL, 0), wcol(D_MODEL, nf),
            wcol(FFN_CONV, 0), wcol(FFN_CONV, nf),
            wcol(1, 0), wcol(1, nf),
            pl.BlockSpec((FFN_TF, D_MODEL), lambda i, m, f: (f, 0)),
        ],
        out_specs=pl.BlockSpec((None, tm, D_MODEL), lambda i, m, f: (i, m, 0)),
        out_shape=jax.ShapeDtypeStruct((b, l, D_MODEL), F32),
        scratch_shapes=[pltpu.VMEM((tm + 2 * FFN_HALO, D_MODEL), BF16), pltpu.VMEM((tm, D_MODEL), F32)],
        compiler_params=_cparams(3),
        name="ffn",
    )(x3, x3, x3, g_ffn, w_up, w_up, conv_w, conv_w, conv_b, conv_b, w_down)


def _layer(x, bias_tab, p):
    b, l, _ = x.shape
    x2 = x.reshape(b * l, D_MODEL)
    packed2, dt2, kt2 = _proj(x2, p["g_mix"], p["w_main"], p["w_dt"], p["dt_bias"], p["w_kt"], p["k_norm_g"])
    packed3 = packed2.reshape(b, l, PROJ_W)
    att = _attention(packed3, kt2, p["q_norm_g"], bias_tab)
    xbc = _ssd_conv(packed3, p["ssd_conv_w"], p["ssd_conv_b"])
    pcs, acsr = _ssd_prep(dt2.reshape(b, l, NDT), p["a_log"])
    y = _ssd(xbc, packed3, pcs, acsr, p["d_skip_e"], p["ssd_norm_g"])
    x1 = _merge(att.reshape(b * l, ATT_W), y.reshape(b * l, SSD_INNER), packed2, x2,
                p["b_gate"], p["w_att_out"], p["w_ssd_out"], p["w_o"])
    return _ffn(x1.reshape(b, l, D_MODEL), p["g_ffn"], p["w_up"], p["ffn_conv_w"], p["ffn_conv_b"],
                p["w_down"])


def kernel(x_prompt, x_sample, g_mix, w_in, b_gate, q_norm_g, k_norm_g, rel_bias, ssd_conv_w, ssd_conv_b,
           dt_bias_f, dt_bias_b, A_log_f, A_log_b, D_skip, ssd_norm_g, w_att_out, w_ssd_out, w_o, g_ffn,
           w_up, ffn_conv_w, ffn_conv_b, w_down):
    depth = g_mix.shape[0]
    xs = [x_prompt, x_sample]
    perm = jnp.asarray(_group_major_perm(), jnp.int32)
    for layer in range(depth):
        w = w_in[layer]
        p = {
            "g_mix": g_mix[layer][None, :],
            "w_main": jnp.concatenate([w[:, :ATT_W], w[:, 2 * ATT_W:DT_COL], w[:, DT_COL + 2 * SSD_HEADS:]],
                                      axis=1).astype(BF16),
            "w_kt": w[:, ATT_W:2 * ATT_W].T.astype(BF16),
            "w_dt": w[:, DT_COL:DT_COL + 2 * SSD_HEADS][:, perm].astype(BF16),
            "dt_bias": jnp.concatenate([dt_bias_f[layer], dt_bias_b[layer]])[perm][None, :],
            "q_norm_g": q_norm_g[layer],
            "k_norm_g": k_norm_g[layer],
            "ssd_conv_w": ssd_conv_w[layer],
            "ssd_conv_b": ssd_conv_b[layer][None, :],
            "a_log": jnp.concatenate([A_log_f[layer], A_log_b[layer]])[perm][None, :],
            "d_skip_e": jnp.repeat(D_skip[layer], SSD_HEAD_DIM)[None, :],
            "ssd_norm_g": ssd_norm_g[layer][None, :],
            "b_gate": b_gate[layer][None, :],
            "w_att_out": w_att_out[layer].astype(BF16),
            "w_ssd_out": w_ssd_out[layer].astype(BF16),
            "w_o": w_o[layer].astype(BF16),
            "g_ffn": g_ffn[layer][None, :],
            "w_up": w_up[layer].astype(BF16),
            "ffn_conv_w": ffn_conv_w[layer],
            "ffn_conv_b": ffn_conv_b[layer][None, :],
            "w_down": w_down[layer].astype(BF16),
        }
        bias_tab = _bias_table(rel_bias[layer])
        xs = [_layer(x, bias_tab, p) for x in xs]
    return tuple(xs)
```
